```python
import math
import jax, jax.numpy as jnp
from jax import lax
import numpy as np

D_MODEL = 2048
BATCH = 16
SEQ = 2048
DEPTH = 2
DEC_BATCH = 2
DEC_SEQ = 8192
PAST_LEN = 128

GRID_W = 64
F_GROUPS = 4
F_GROUP_DIM = 128
F_W = F_GROUPS * F_GROUP_DIM
C_W = 512
CONV_K = 31
CONV_PAD = CONV_K // 2
N_HEADS = 8
HEAD_DIM = 128
A_W = N_HEADS * HEAD_DIM
WIN_R = 8
WIN_C = 16
KEY_COLS = 2 * WIN_C
N_COL_BLOCKS = GRID_W // WIN_C
N_BRANCH = 3
OFF_F = 0
OFF_CA = OFF_F + F_W
OFF_CG = OFF_CA + C_W
OFF_Q = OFF_CG + C_W
OFF_K = OFF_Q + A_W
OFF_V = OFF_K + A_W
OFF_G = OFF_V + A_W
IN_COLS = OFF_G + N_BRANCH * D_MODEL
D_FF = 4 * D_MODEL
EPS = 1e-6
NEG_INF = -1e30

kernel_name = 'hybrid_fnet_conformer_natten_encoder'


def rms_norm(x, g):
    xf = x.astype(jnp.float32)
    y = xf * lax.rsqrt(jnp.mean(xf * xf, axis=-1, keepdims=True) + EPS)
    return (y * g.astype(jnp.float32)).astype(x.dtype)


def layer_norm(x, g, b):
    xf = x.astype(jnp.float32)
    mu = jnp.mean(xf, axis=-1, keepdims=True)
    xc = xf - mu
    y = xc * lax.rsqrt(jnp.mean(xc * xc, axis=-1, keepdims=True) + EPS)
    return (y * g.astype(jnp.float32) + b.astype(jnp.float32)).astype(x.dtype)


def fourier_mix(u):
    b, s, _ = u.shape
    uf = u.astype(jnp.float32).reshape(b, s, F_GROUPS, F_GROUP_DIM)
    y = jnp.fft.fft2(uf, axes=(1, 3), norm='ortho').real
    return y.reshape(b, s, F_W).astype(u.dtype)


def conv_module(a, g, dw, db, ln_g, ln_b):
    u = a * jax.nn.sigmoid(g)
    y = lax.conv_general_dilated(
        u, dw[:, None, :].astype(u.dtype), window_strides=(1,),
        padding=[(CONV_PAD, CONV_PAD)],
        dimension_numbers=('NWC', 'WIO', 'NWC'),
        feature_group_count=C_W) + db
    return jax.nn.silu(layer_norm(y, ln_g, ln_b))


def na_column_tables():
    col_starts = tuple(min(max(j * WIN_C - WIN_C // 2, 0), GRID_W - KEY_COLS)
                       for j in range(N_COL_BLOCKS))
    j = np.arange(N_COL_BLOCKS)[:, None, None]
    qi = np.arange(WIN_C)[None, :, None]
    kk = np.arange(KEY_COLS)[None, None, :]
    c = j * WIN_C + qi
    start = np.clip(c - WIN_C // 2, 0, GRID_W - WIN_C)
    kc = np.asarray(col_starts)[:, None, None] + kk
    mask = (kc >= start) & (kc < start + WIN_C)
    dc = np.clip(kc - c, -(WIN_C - 1), WIN_C - 1) + (WIN_C - 1)
    return col_starts, mask[:, :, None, :], dc[:, :, None, :].astype(np.int32)


def neighbourhood_attention(q, k, v, rpb):
    b, s = q.shape[0], q.shape[1]
    rows = s // GRID_W
    kr = min(WIN_R, rows)
    scale = 1.0 / math.sqrt(HEAD_DIM)
    qg = q.reshape(b, rows, GRID_W, N_HEADS, HEAD_DIM)
    kg = k.reshape(b, rows, GRID_W, N_HEADS, HEAD_DIM)
    vg = v.reshape(b, rows, GRID_W, N_HEADS, HEAD_DIM)
    col_starts, mask, dc_idx = na_column_tables()

    def row_step(r):
        rs = jnp.clip(r - kr // 2, 0, rows - kr)
        k_rows = lax.dynamic_slice_in_dim(kg, rs, kr, axis=1)
        v_rows = lax.dynamic_slice_in_dim(vg, rs, kr, axis=1)
        kb = jnp.stack([k_rows[:, :, c0:c0 + KEY_COLS] for c0 in col_starts], axis=1)
        vb = jnp.stack([v_rows[:, :, c0:c0 + KEY_COLS] for c0 in col_starts], axis=1)
        qr = lax.dynamic_index_in_dim(qg, r, axis=1, keepdims=False)
        qr = qr.reshape(b, N_COL_BLOCKS, WIN_C, N_HEADS, HEAD_DIM)
        sc = jnp.einsum('bjqhd,bjrkhd->bhjqrk', qr, kb).astype(jnp.float32) * scale
        dr_idx = rs + jnp.arange(kr) - r + (WIN_R - 1)
        bias = rpb[:, dr_idx[None, None, :, None], dc_idx]
        sc = jnp.where(mask, sc + bias.astype(jnp.float32)[None], NEG_INF)
        p = jax.nn.softmax(sc, axis=(-2, -1))
        o = jnp.einsum('bhjqrk,bjrkhd->bjqhd', p.astype(vb.dtype), vb)
        return o.reshape(b, GRID_W, A_W)

    out = lax.map(row_step, jnp.arange(rows))
    return out.transpose(1, 0, 2, 3).reshape(b, s, A_W)


def encoder_layer(x, n1, w_in, b_g, w_f, dw, db, ln_g, ln_b, w_c, qg_, kg_, rpb, w_a, w_o, n2, w1, w2):
    b, s, _ = x.shape
    h = rms_norm(x, n1)
    z = jnp.einsum('bsd,dn->bsn', h, w_in)
    y_f = jnp.einsum('bsc,cd->bsd', fourier_mix(z[..., OFF_F:OFF_CA]), w_f)
    u_c = conv_module(z[..., OFF_CA:OFF_CG], z[..., OFF_CG:OFF_Q], dw, db, ln_g, ln_b)
    y_c = jnp.einsum('bsc,cd->bsd', u_c, w_c)
    q = rms_norm(z[..., OFF_Q:OFF_K].reshape(b, s, N_HEADS, HEAD_DIM), qg_)
    k = rms_norm(z[..., OFF_K:OFF_V].reshape(b, s, N_HEADS, HEAD_DIM), kg_)
    v = z[..., OFF_V:OFF_G].reshape(b, s, N_HEADS, HEAD_DIM)
    y_a = jnp.einsum('bsc,cd->bsd', neighbourhood_attention(q, k, v, rpb), w_a)
    gates = jax.nn.sigmoid(z[..., OFF_G:] + b_g).reshape(b, s, N_BRANCH, D_MODEL)
    merged = gates[:, :, 0] * y_f + gates[:, :, 1] * y_c + gates[:, :, 2] * y_a
    x = x + jnp.einsum('bsd,de->bse', merged, w_o)
    h2 = rms_norm(x, n2)
    a = jax.nn.relu(jnp.einsum('bsd,df->bsf', h2, w1))
    return x + jnp.einsum('bsf,fd->bsd', a * a, w2)


def setup_inputs(seed: int = 0) -> dict:
    key = jax.random.key(seed)
    ks = jax.random.split(key, 20)
    f32 = jnp.float32
    nrm = lambda k, shape, sc: jax.random.normal(k, shape, f32) * sc
    return {
        'x_prompt': nrm(ks[0], (BATCH, SEQ, D_MODEL), 1.0),
        'x_sample': nrm(ks[1], (DEC_BATCH, DEC_SEQ, D_MODEL), 1.0),
        'norm1_g': 1.0 + nrm(ks[2], (DEPTH, D_MODEL), 0.02),
        'w_in': nrm(ks[3], (DEPTH, D_MODEL, IN_COLS), D_MODEL ** -0.5),
        'b_gate': nrm(ks[4], (DEPTH, N_BRANCH * D_MODEL), 0.1),
        'w_fourier': nrm(ks[5], (DEPTH, F_W, D_MODEL), F_W ** -0.5),
        'conv_dw': nrm(ks[6], (DEPTH, CONV_K, C_W), CONV_K ** -0.5),
        'conv_db': nrm(ks[7], (DEPTH, C_W), 0.02),
        'conv_ln_g': 1.0 + nrm(ks[8], (DEPTH, C_W), 0.02),
        'conv_ln_b': nrm(ks[9], (DEPTH, C_W), 0.02),
        'w_conv_out': nrm(ks[10], (DEPTH, C_W, D_MODEL), C_W ** -0.5),
        'q_norm_g': 1.0 + nrm(ks[11], (DEPTH, HEAD_DIM), 0.02),
        'k_norm_g': 1.0 + nrm(ks[12], (DEPTH, HEAD_DIM), 0.02),
        'rpb': nrm(ks[13], (DEPTH, N_HEADS, 2 * WIN_R - 1, 2 * WIN_C - 1), 0.1),
        'w_attn_out': nrm(ks[14], (DEPTH, A_W, D_MODEL), A_W ** -0.5),
        'w_out': nrm(ks[15], (DEPTH, D_MODEL, D_MODEL), D_MODEL ** -0.5),
        'norm2_g': 1.0 + nrm(ks[16], (DEPTH, D_MODEL), 0.02),
        'w_mlp_in': nrm(ks[17], (DEPTH, D_MODEL, D_FF), D_MODEL ** -0.5),
        'w_mlp_out': nrm(ks[18], (DEPTH, D_FF, D_MODEL), D_FF ** -0.5),
    }


def reference(x_prompt, x_sample, norm1_g, w_in, b_gate, w_fourier, conv_dw, conv_db, conv_ln_g, conv_ln_b,
              w_conv_out, q_norm_g, k_norm_g, rpb, w_attn_out, w_out, norm2_g, w_mlp_in, w_mlp_out):
    def trunk(x):
        for l in range(DEPTH):
            x = encoder_layer(x, norm1_g[l], w_in[l], b_gate[l], w_fourier[l], conv_dw[l], conv_db[l],
                              conv_ln_g[l], conv_ln_b[l], w_conv_out[l], q_norm_g[l], k_norm_g[l], rpb[l],
                              w_attn_out[l], w_out[l], norm2_g[l], w_mlp_in[l], w_mlp_out[l])
        return x
    y_prompt = trunk(x_prompt)
    y_sample = trunk(x_sample)
    return (y_prompt, y_sample)
```

```python
import functools
import math

import numpy as np
import jax
import jax.numpy as jnp
from jax import lax
from jax.experimental import pallas as pl
from jax.experimental.pallas import tpu as pltpu

D_MODEL = 2048
GRID_W = 64
F_GROUPS = 4
F_GROUP_DIM = 128
F_W = F_GROUPS * F_GROUP_DIM
C_W = 512
CONV_K = 31
CONV_PAD = CONV_K // 2
N_HEADS = 8
HEAD_DIM = 128
A_W = N_HEADS * HEAD_DIM
WIN_R = 8
WIN_C = 16
N_BRANCH = 3
OFF_F = 0
OFF_CA = OFF_F + F_W
OFF_CG = OFF_CA + C_W
OFF_Q = OFF_CG + C_W
OFF_K = OFF_Q + A_W
OFF_V = OFF_K + A_W
OFF_G = OFF_V + A_W
IN_COLS = OFF_G + N_BRANCH * D_MODEL
D_FF = 4 * D_MODEL
EPS = 1e-6
NEG_INF = -1e30

BF16 = jnp.bfloat16
F32 = jnp.float32

V7X_VMEM_BYTES = 64 * 1024 * 1024
V7X_LANES = 128

TOKEN_TILE = 512
MERGE_COLS = 512
MLP_COLS = 1024
CONV_ROWS = 256
CONV_HALO = 16
CONV_CHUNK = 16
Q_ROWS = 4
K_GROUPS = 3
Q_TOK = Q_ROWS * GRID_W


def _vmem_limit(nbytes):
    return int(min(nbytes * 5 // 4 + (8 << 20), V7X_VMEM_BYTES - (4 << 20)))


def _rms_norm_bf16(x, gain):
    ms = jnp.mean(x * x, axis=-1, keepdims=True)
    return ((x * lax.rsqrt(ms + EPS)) * gain).astype(BF16)


def _in_proj_kernel(x_ref, g1_ref, w_ref, qg_ref, kg_ref, f_ref, glu_ref, q_ref, k_ref, v_ref):
    h = _rms_norm_bf16(x_ref[...], g1_ref[...])

    def proj(c0, n):
        return jnp.dot(h, w_ref[:, c0:c0 + n], preferred_element_type=F32)

    f_ref[...] = proj(OFF_F, F_W).astype(BF16)
    a = proj(OFF_CA, C_W)
    g = proj(OFF_CG, C_W)
    glu_ref[...] = a * jax.nn.sigmoid(g)

    def head_norm(z, gain, scale, o_ref):
        for hd in range(N_HEADS):
            hs = slice(hd * HEAD_DIM, (hd + 1) * HEAD_DIM)
            zh = z[:, hs]
            ms = jnp.mean(zh * zh, axis=-1, keepdims=True)
            y = (zh * lax.rsqrt(ms + EPS)) * gain
            if scale != 1.0:
                y = y * scale
            o_ref[:, hs] = y.astype(BF16)

    head_norm(proj(OFF_Q, A_W), qg_ref[...], 1.0 / math.sqrt(HEAD_DIM), q_ref)
    head_norm(proj(OFF_K, A_W), kg_ref[...], 1.0, k_ref)
    v_ref[...] = proj(OFF_V, A_W).astype(BF16)


def _in_proj(x, g1, w_in, qg, kg):
    t = x.shape[0]
    tm = TOKEN_TILE
    row = lambda i: (i, 0)
    fixed = lambda i: (0, 0)
    vmem = (2 * tm * D_MODEL * 4 + D_MODEL * OFF_G * 2
            + 2 * tm * (F_W * 2 + C_W * 4 + 3 * A_W * 2) + 2 * tm * A_W * 4)
    return pl.pallas_call(
        _in_proj_kernel,
        grid=(t // tm,),
        in_specs=[
            pl.BlockSpec((tm, D_MODEL), row),
            pl.BlockSpec((1, D_MODEL), fixed),
            pl.BlockSpec((D_MODEL, OFF_G), fixed, pipeline_mode=pl.Buffered(1)),
            pl.BlockSpec((1, HEAD_DIM), fixed),
            pl.BlockSpec((1, HEAD_DIM), fixed),
        ],
        out_specs=[
            pl.BlockSpec((tm, F_W), row),
            pl.BlockSpec((tm, C_W), row),
            pl.BlockSpec((tm, A_W), row),
            pl.BlockSpec((tm, A_W), row),
            pl.BlockSpec((tm, A_W), row),
        ],
        out_shape=[
            jax.ShapeDtypeStruct((t, F_W), BF16),
            jax.ShapeDtypeStruct((t, C_W), F32),
            jax.ShapeDtypeStruct((t, A_W), BF16),
            jax.ShapeDtypeStruct((t, A_W), BF16),
            jax.ShapeDtypeStruct((t, A_W), BF16),
        ],
        compiler_params=pltpu.CompilerParams(
            dimension_semantics=("arbitrary",), vmem_limit_bytes=_vmem_limit(vmem)),
        name="in_proj",
    )(x, g1, w_in, qg, kg)


def _dft_factors(s):
    s1 = 256 if s >= 8192 else 128
    return s1, s // s1


@functools.lru_cache(maxsize=None)
def _dft_tables(s):
    s1, s2 = _dft_factors(s)
    n1 = np.arange(s1)
    ang1 = 2.0 * np.pi * np.outer(n1, n1) / s1
    f1 = np.concatenate([np.cos(ang1), -np.sin(ang1)], axis=0) / np.sqrt(s1)
    k1 = np.arange(s1)[:, None, None]
    k2 = np.arange(s2)[None, :, None]
    n2 = np.arange(s2)[None, None, :]
    ang2 = 2.0 * np.pi * ((n2 * (k1 + s1 * k2)) % s) / s
    gr = np.cos(ang2) / np.sqrt(s2)
    gi = -np.sin(ang2) / np.sqrt(s2)
    g2 = np.concatenate([np.concatenate([gr, -gi], axis=2),
                         np.concatenate([gi, gr], axis=2)], axis=1)
    c = np.arange(F_GROUP_DIM)
    ang3 = 2.0 * np.pi * np.outer(c, c) / F_GROUP_DIM
    cs = np.concatenate([np.cos(ang3), np.sin(ang3)], axis=0) / np.sqrt(F_GROUP_DIM)
    as_bf16 = lambda a: jnp.asarray(a.astype(np.float32)).astype(BF16)
    return as_bf16(f1), as_bf16(g2), as_bf16(cs)


def _dft_seq1_kernel(f1_ref, x_ref, a_ref):
    a_ref[0] = jnp.dot(f1_ref[...], x_ref[0], preferred_element_type=F32).astype(BF16)


def _dft_seq2_kernel(g2_ref, cs_ref, a_ref, o_ref, *, kb, s2):
    a2 = jnp.concatenate([a_ref[0, 0], a_ref[0, 1]], axis=1)
    w = jnp.einsum("kmn,knc->kmc", g2_ref[...], a2, preferred_element_type=F32)
    wr = w[:, :s2, :].astype(BF16)
    wi = w[:, s2:, :].astype(BF16)
    for g in range(F_GROUPS):
        gs = slice(g * F_GROUP_DIM, (g + 1) * F_GROUP_DIM)
        lhs = jnp.concatenate([wr[:, :, gs], wi[:, :, gs]], axis=-1).reshape(kb * s2, 2 * F_GROUP_DIM)
        og = jnp.dot(lhs, cs_ref[...], preferred_element_type=F32).astype(BF16)
        og = og.reshape(kb, s2, F_GROUP_DIM)
        for k in range(kb):
            o_ref[0, :, k * F_W + g * F_GROUP_DIM:k * F_W + (g + 1) * F_GROUP_DIM] = og[k]


def _fourier_mix(u, b, s):
    s1, s2 = _dft_factors(s)
    f1, g2, cs = _dft_tables(s)
    nb = min(s2, 8)
    x = u.reshape(b, s1, s2 * F_W)
    a = pl.pallas_call(
        _dft_seq1_kernel,
        grid=(b, s2 // nb),
        in_specs=[
            pl.BlockSpec((2 * s1, s1), lambda i, j: (0, 0)),
            pl.BlockSpec((1, s1, nb * F_W), lambda i, j: (i, 0, j)),
        ],
        out_specs=pl.BlockSpec((1, 2 * s1, nb * F_W), lambda i, j: (i, 0, j)),
        out_shape=jax.ShapeDtypeStruct((b, 2 * s1, s2 * F_W), BF16),
        compiler_params=pltpu.CompilerParams(
            dimension_semantics=("arbitrary", "arbitrary"),
            vmem_limit_bytes=_vmem_limit(2 * 3 * s1 * nb * F_W * 2 + 2 * s1 * nb * F_W * 4)),
        name="dft_seq1",
    )(f1, x)
    a = a.reshape(b, 2, s1, s2, F_W)
    kb = 16
    out = pl.pallas_call(
        functools.partial(_dft_seq2_kernel, kb=kb, s2=s2),
        grid=(b, s1 // kb),
        in_specs=[
            pl.BlockSpec((kb, 2 * s2, 2 * s2), lambda i, j: (j, 0, 0)),
            pl.BlockSpec((2 * F_GROUP_DIM, F_GROUP_DIM), lambda i, j: (0, 0)),
            pl.BlockSpec((1, 2, kb, s2, F_W), lambda i, j: (i, 0, j, 0, 0)),
        ],
        out_specs=pl.BlockSpec((1, s2, kb * F_W), lambda i, j: (i, 0, j)),
        out_shape=jax.ShapeDtypeStruct((b, s2, s1 * F_W), BF16),
        compiler_params=pltpu.CompilerParams(
            dimension_semantics=("arbitrary", "arbitrary"),
            vmem_limit_bytes=_vmem_limit(2 * 3 * kb * s2 * F_W * 2 + 4 * kb * s2 * F_W * 4)),
        name="dft_seq2",
    )(g2, cs, a)
    return out.reshape(b * s, F_W)


def _conv_kernel(prev_ref, cur_ref, next_ref, dw_ref, db_ref, lg_ref, lb_ref, o_ref, pad_ref):
    i = pl.program_id(1)
    last = pl.num_programs(1) - 1
    pad_ref[0:CONV_HALO, :] = jnp.where(i > 0, prev_ref[0, 0], 0.0)
    pad_ref[CONV_HALO:CONV_HALO + CONV_ROWS, :] = cur_ref[0]
    pad_ref[CONV_HALO + CONV_ROWS:, :] = jnp.where(i < last, next_ref[0, 0], 0.0)
    first = CONV_HALO - CONV_PAD
    for c in range(CONV_ROWS // CONV_CHUNK):
        r0 = c * CONV_CHUNK
        acc = jnp.broadcast_to(db_ref[...], (CONV_CHUNK, C_W))
        for t in range(CONV_K):
            acc = acc + dw_ref[t] * pad_ref[pl.ds(r0 + first + t, CONV_CHUNK), :]
        mu = jnp.mean(acc, axis=-1, keepdims=True)
        xc = acc - mu
        var = jnp.mean(xc * xc, axis=-1, keepdims=True)
        y = (xc * lax.rsqrt(var + EPS)) * lg_ref[...] + lb_ref[...]
        o_ref[0, r0:r0 + CONV_CHUNK, :] = (y * jax.nn.sigmoid(y)).astype(BF16)


def _conv_module(glu, b, s, dw, db, lg, lb):
    nblk = s // CONV_ROWS
    per = CONV_ROWS // CONV_HALO
    x3 = glu.reshape(b, s, C_W)
    x4 = glu.reshape(b, s // CONV_HALO, CONV_HALO, C_W)
    dwb = jnp.broadcast_to(dw[:, None, :], (CONV_K, CONV_CHUNK, C_W))
    vec = lambda i, j: (0, 0)
    out = pl.pallas_call(
        _conv_kernel,
        grid=(b, nblk),
        in_specs=[
            pl.BlockSpec((1, 1, CONV_HALO, C_W), lambda i, j: (i, jnp.maximum(j * per - 1, 0), 0, 0)),
            pl.BlockSpec((1, CONV_ROWS, C_W), lambda i, j: (i, j, 0)),
            pl.BlockSpec((1, 1, CONV_HALO, C_W),
                         lambda i, j: (i, jnp.minimum((j + 1) * per, s // CONV_HALO - 1), 0, 0)),
            pl.BlockSpec((CONV_K, CONV_CHUNK, C_W), lambda i, j: (0, 0, 0)),
            pl.BlockSpec((1, C_W), vec),
            pl.BlockSpec((1, C_W), vec),
            pl.BlockSpec((1, C_W), vec),
        ],
        out_specs=pl.BlockSpec((1, CONV_ROWS, C_W), lambda i, j: (i, j, 0)),
        out_shape=jax.ShapeDtypeStruct((b, s, C_W), BF16),
        scratch_shapes=[pltpu.VMEM((CONV_ROWS + 2 * CONV_HALO, C_W), F32)],
        compiler_params=pltpu.CompilerParams(
            dimension_semantics=("arbitrary", "arbitrary"),
            vmem_limit_bytes=_vmem_limit(4 * CONV_ROWS * C_W * 4 + 2 * CONV_K * CONV_CHUNK * C_W * 4)),
        name="conv",
    )(x4, x3, x4, dwb, db, lg, lb)
    return out.reshape(b * s, C_W)


@functools.lru_cache(maxsize=None)
def _natten_index_tables():
    n_key_rows = K_GROUPS * Q_ROWS
    i = np.arange(Q_ROWS)[:, None, None, None]
    qc = np.arange(GRID_W)[None, :, None, None]
    rr = np.arange(n_key_rows)[None, None, :, None]
    kc = np.arange(GRID_W)[None, None, None, :]
    lo = (np.zeros_like(i), i, np.full_like(i, n_key_rows - WIN_R))
    dr_off = (WIN_R - 1, WIN_R - 1 - Q_ROWS, WIN_R - 1 - 2 * Q_ROWS)
    start = np.clip(qc - WIN_C // 2, 0, GRID_W - WIN_C)
    col_ok = (kc >= start) & (kc < start + WIN_C)
    dc = np.clip(kc - qc, -(WIN_C - 1), WIN_C - 1) + (WIN_C - 1)
    shape = (Q_ROWS, GRID_W, n_key_rows, GRID_W)
    dr_all, ok_all = [], []
    for p in range(3):
        row_ok = (rr >= lo[p]) & (rr < lo[p] + WIN_R)
        dr = np.clip(rr - i + dr_off[p], 0, 2 * WIN_R - 2)
        dr_all.append(np.broadcast_to(dr, shape).reshape(Q_TOK, n_key_rows * GRID_W))
        ok_all.append(np.broadcast_to(row_ok & col_ok, shape).reshape(Q_TOK, n_key_rows * GRID_W))
    dc = np.broadcast_to(dc, shape).reshape(Q_TOK, n_key_rows * GRID_W)
    return np.stack(dr_all).astype(np.int32), dc.astype(np.int32), np.stack(ok_all)


def _natten_bias_table(rpb):
    dr, dc, ok = _natten_index_tables()
    tab = rpb[:, dr, dc[None]]
    tab = jnp.where(ok[None], tab, NEG_INF)
    return jnp.transpose(tab, (1, 0, 2, 3))


def _natten_kernel(q_ref, k0_ref, k1_ref, k2_ref, v0_ref, v1_ref, v2_ref, tab_ref, o_ref):
    k_refs = (k0_ref, k1_ref, k2_ref)
    v_refs = (v0_ref, v1_ref, v2_ref)
    for hd in range(N_HEADS):
        hs = slice(hd * HEAD_DIM, (hd + 1) * HEAD_DIM)
        qh = q_ref[:, hs]
        sc = []
        for g in range(K_GROUPS):
            s = lax.dot_general(qh, k_refs[g][:, hs], (((1,), (1,)), ((), ())),
                                preferred_element_type=F32)
            t = tab_ref[0, hd, :, g * Q_TOK:(g + 1) * Q_TOK]
            sc.append(jnp.where(t > 0.5 * NEG_INF, s + t, NEG_INF))
        m = sc[0].max(axis=-1, keepdims=True)
        for g in range(1, K_GROUPS):
            m = jnp.maximum(m, sc[g].max(axis=-1, keepdims=True))
        denom = None
        o = None
        for g in range(K_GROUPS):
            p = jnp.exp(sc[g] - m)
            ps = p.sum(axis=-1, keepdims=True)
            pv = jnp.dot(p.astype(BF16), v_refs[g][:, hs], preferred_element_type=F32)
            denom = ps if denom is None else denom + ps
            o = pv if o is None else o + pv
        o_ref[:, hs] = (o / denom).astype(BF16)


def _natten(q, k, v, tab, b, s):
    ng = s // Q_TOK
    assert ng >= K_GROUPS

    def q_map(i, j):
        return (i * ng + j, 0)

    def kv_map(off):
        return lambda i, j: (i * ng + jnp.clip(j - 1, 0, ng - K_GROUPS) + off, 0)

    def tab_map(i, j):
        return (jnp.where(j == 0, 0, jnp.where(j == ng - 1, 2, 1)), 0, 0, 0)

    blk = (Q_TOK, A_W)
    vmem = 2 * (2 + 2 * K_GROUPS) * Q_TOK * A_W * 2 + 2 * N_HEADS * Q_TOK * K_GROUPS * Q_TOK * 4
    return pl.pallas_call(
        _natten_kernel,
        grid=(b, ng),
        in_specs=[pl.BlockSpec(blk, q_map)]
        + [pl.BlockSpec(blk, kv_map(g)) for g in range(K_GROUPS)]
        + [pl.BlockSpec(blk, kv_map(g)) for g in range(K_GROUPS)]
        + [pl.BlockSpec((1, N_HEADS, Q_TOK, K_GROUPS * Q_TOK), tab_map)],
        out_specs=pl.BlockSpec(blk, q_map),
        out_shape=jax.ShapeDtypeStruct((b * s, A_W), BF16),
        compiler_params=pltpu.CompilerParams(
            dimension_semantics=("arbitrary", "arbitrary"), vmem_limit_bytes=_vmem_limit(vmem)),
        name="natten",
    )(q, k, k, k, v, v, v, tab)


def _merge_kernel(x_ref, g1_ref, fm_ref, uc_ref, at_ref, wg0_ref, wg1_ref, wg2_ref,
                  bg0_ref, bg1_ref, bg2_ref, wf_ref, wc_ref, wa_ref, wo_ref, o_ref, h_ref):
    @pl.when(pl.program_id(1) == 0)
    def _():
        x = x_ref[...]
        h_ref[...] = _rms_norm_bf16(x, g1_ref[...])
        o_ref[...] = x

    h = h_ref[...]

    def gate(w_ref, b_ref):
        return jax.nn.sigmoid(jnp.dot(h, w_ref[...], preferred_element_type=F32) + b_ref[...])

    merged = gate(wg0_ref, bg0_ref) * jnp.dot(fm_ref[...], wf_ref[...], preferred_element_type=F32)
    merged += gate(wg1_ref, bg1_ref) * jnp.dot(uc_ref[...], wc_ref[...], preferred_element_type=F32)
    merged += gate(wg2_ref, bg2_ref) * jnp.dot(at_ref[...], wa_ref[...], preferred_element_type=F32)
    o_ref[...] += jnp.dot(merged.astype(BF16), wo_ref[...], preferred_element_type=F32)


def _merge(x, g1, fm, uc, at, w_in, b_gate, w_f, w_c, w_a, w_o):
    t = x.shape[0]
    tm, tn = TOKEN_TILE, MERGE_COLS
    nj = D_MODEL // tn
    gate_blk = OFF_G // tn
    row = lambda i, j: (i, 0)
    col = lambda i, j: (0, j)
    gate_specs = [pl.BlockSpec((D_MODEL, tn), lambda i, j, br=br: (0, gate_blk + br * nj + j))
                  for br in range(N_BRANCH)]
    bias_specs = [pl.BlockSpec((1, tn), lambda i, j, br=br: (0, br * nj + j)) for br in range(N_BRANCH)]
    vmem = (4 * tm * D_MODEL * 4 + 2 * tm * (F_W + C_W + A_W) * 2 + tm * D_MODEL * 2
            + 2 * (3 * D_MODEL + F_W + C_W + A_W + D_MODEL) * tn * 2 + 6 * tm * tn * 4)
    return pl.pallas_call(
        _merge_kernel,
        grid=(t // tm, nj),
        in_specs=[
            pl.BlockSpec((tm, D_MODEL), row),
            pl.BlockSpec((1, D_MODEL), lambda i, j: (0, 0)),
            pl.BlockSpec((tm, F_W), row),
            pl.BlockSpec((tm, C_W), row),
            pl.BlockSpec((tm, A_W), row),
            *gate_specs,
            *bias_specs,
            pl.BlockSpec((F_W, tn), col),
            pl.BlockSpec((C_W, tn), col),
            pl.BlockSpec((A_W, tn), col),
            pl.BlockSpec((tn, D_MODEL), lambda i, j: (j, 0)),
        ],
        out_specs=pl.BlockSpec((tm, D_MODEL), row),
        out_shape=jax.ShapeDtypeStruct((t, D_MODEL), F32),
        scratch_shapes=[pltpu.VMEM((tm, D_MODEL), BF16)],
        compiler_params=pltpu.CompilerParams(
            dimension_semantics=("arbitrary", "arbitrary"), vmem_limit_bytes=_vmem_limit(vmem)),
        name="merge",
    )(x, g1, fm, uc, at, w_in, w_in, w_in, b_gate, b_gate, b_gate, w_f, w_c, w_a, w_o)


def _mlp_kernel(x_ref, g2_ref, w1_ref, w2_ref, o_ref, h_ref):
    @pl.when(pl.program_id(1) == 0)
    def _():
        x = x_ref[...]
        h_ref[...] = _rms_norm_bf16(x, g2_ref[...])
        o_ref[...] = x

    a = jnp.maximum(jnp.dot(h_ref[...], w1_ref[...], preferred_element_type=F32), 0.0)
    o_ref[...] += jnp.dot((a * a).astype(BF16), w2_ref[...], preferred_element_type=F32)


def _mlp(x, g2, w1, w2):
    t = x.shape[0]
    tm, tf = TOKEN_TILE, MLP_COLS
    row = lambda i, j: (i, 0)
    vmem = 4 * tm * D_MODEL * 4 + tm * D_MODEL * 2 + 2 * 2 * D_MODEL * tf * 2 + 2 * tm * tf * 4
    return pl.pallas_call(
        _mlp_kernel,
        grid=(t // tm, D_FF // tf),
        in_specs=[
            pl.BlockSpec((tm, D_MODEL), row),
            pl.BlockSpec((1, D_MODEL), lambda i, j: (0, 0)),
            pl.BlockSpec((D_MODEL, tf), lambda i, j: (0, j)),
            pl.BlockSpec((tf, D_MODEL), lambda i, j: (j, 0)),
        ],
        out_specs=pl.BlockSpec((tm, D_MODEL), row),
        out_shape=jax.ShapeDtypeStruct((t, D_MODEL), F32),
        scratch_shapes=[pltpu.VMEM((tm, D_MODEL), BF16)],
        compiler_params=pltpu.CompilerParams(
            dimension_semantics=("arbitrary", "arbitrary"), vmem_limit_bytes=_vmem_limit(vmem)),
        name="mlp",
    )(x, g2, w1, w2)


def _layer(x, b, s, p):
    f, glu, q, k, v = _in_proj(x, p["n1"], p["w_in"], p["qg"], p["kg"])
    fm = _fourier_mix(f, b, s)
    uc = _conv_module(glu, b, s, p["dw"], p["db"], p["ln_g"], p["ln_b"])
    at = _natten(q, k, v, p["tab"], b, s)
    x = _merge(x, p["n1"], fm, uc, at, p["w_in"], p["b_g"], p["w_f"], p["w_c"], p["w_a"], p["w_o"])
    return _mlp(x, p["n2"], p["w1"], p["w2"])


def kernel(x_prompt, x_sample, norm1_g, w_in, b_gate, w_fourier, conv_dw, conv_db, conv_ln_g, conv_ln_b,
           w_conv_out, q_norm_g, k_norm_g, rpb, w_attn_out, w_out, norm2_g, w_mlp_in, w_mlp_out):
    depth = w_in.shape[0]
    layers = []
    for l in range(depth):
        layers.append(dict(
            n1=norm1_g[l][None, :], n2=norm2_g[l][None, :],
            w_in=w_in[l].astype(BF16), b_g=b_gate[l][None, :],
            w_f=w_fourier[l].astype(BF16), w_c=w_conv_out[l].astype(BF16),
            w_a=w_attn_out[l].astype(BF16), w_o=w_out[l].astype(BF16),
            w1=w_mlp_in[l].astype(BF16), w2=w_mlp_out[l].astype(BF16),
            dw=conv_dw[l], db=conv_db[l][None, :], ln_g=conv_ln_g[l][None, :], ln_b=conv_ln_b[l][None, :],
            qg=q_norm_g[l][None, :], kg=k_norm_g[l][None, :],
            tab=_natten_bias_table(rpb[l]),
        ))

    def trunk(x):
        b, s, d = x.shape
        y = x.reshape(b * s, d)
        for p in layers:
            y = _layer(y, b, s, p)
        return y.reshape(b, s, d)

    return (trunk(x_prompt), trunk(x_sample))
```

```python
import functools
import math

import numpy as np
import jax
import jax.numpy as jnp
from jax import lax
from jax.experimental import pallas as pl
from jax.experimental.pallas import tpu as pltpu

D_MODEL = 2048
GRID_W = 64
F_GROUPS = 4
F_GROUP_DIM = 128
F_W = F_GROUPS * F_GROUP_DIM
C_W = 512
CONV_K = 31
CONV_PAD = CONV_K // 2
N_HEADS = 8
HEAD_DIM = 128
A_W = N_HEADS * HEAD_DIM
WIN_R = 8
WIN_C = 16
N_BRANCH = 3
OFF_F = 0
OFF_CA = OFF_F + F_W
OFF_CG = OFF_CA + C_W
OFF_Q = OFF_CG + C_W
OFF_K = OFF_Q + A_W
OFF_V = OFF_K + A_W
OFF_G = OFF_V + A_W
IN_COLS = OFF_G + N_BRANCH * D_MODEL
D_FF = 4 * D_MODEL
EPS = 1e-6
NEG_INF = -1e30

BF16 = jnp.bfloat16
F32 = jnp.float32

V7X_VMEM_BYTES = 64 * 1024 * 1024
V7X_LANES = 128

TOKEN_TILE = 512
MERGE_COLS = 512
MLP_COLS = 1024
CONV_ROWS = 256
CONV_HALO = 16
CONV_CHUNK = 16
Q_ROWS = 4
K_GROUPS = 3
Q_TOK = Q_ROWS * GRID_W


def _vmem_limit(nbytes):
    return int(min(nbytes * 5 // 4 + (8 << 20), V7X_VMEM_BYTES - (4 << 20)))


def _rms_norm_bf16(x, gain):
    ms = jnp.mean(x * x, axis=-1, keepdims=True)
    return ((x * lax.rsqrt(ms + EPS)) * gain).astype(BF16)


def _in_proj_kernel(x_ref, g1_ref, w_ref, qg_ref, kg_ref, f_ref, glu_ref, q_ref, k_ref, v_ref):
    h = _rms_norm_bf16(x_ref[...], g1_ref[...])

    def proj(c0, n):
        return jnp.dot(h, w_ref[:, c0:c0 + n], preferred_element_type=F32)

    f_ref[...] = proj(OFF_F, F_W).astype(BF16)
    a = proj(OFF_CA, C_W)
    g = proj(OFF_CG, C_W)
    glu_ref[...] = a * jax.nn.sigmoid(g)

    def head_norm(z, gain, scale, o_ref):
        for hd in range(N_HEADS):
            hs = slice(hd * HEAD_DIM, (hd + 1) * HEAD_DIM)
            zh = z[:, hs]
            ms = jnp.mean(zh * zh, axis=-1, keepdims=True)
            y = (zh * lax.rsqrt(ms + EPS)) * gain
            if scale != 1.0:
                y = y * scale
            o_ref[:, hs] = y.astype(BF16)

    head_norm(proj(OFF_Q, A_W), qg_ref[...], 1.0 / math.sqrt(HEAD_DIM), q_ref)
    head_norm(proj(OFF_K, A_W), kg_ref[...], 1.0, k_ref)
    v_ref[...] = proj(OFF_V, A_W).astype(BF16)


def _in_proj(x, g1, w_in, qg, kg):
    t = x.shape[0]
    tm = TOKEN_TILE
    row = lambda i: (i, 0)
    fixed = lambda i: (0, 0)
    vmem = (2 * tm * D_MODEL * 4 + D_MODEL * OFF_G * 2
            + 2 * tm * (F_W * 2 + C_W * 4 + 3 * A_W * 2) + 2 * tm * A_W * 4)
    return pl.pallas_call(
        _in_proj_kernel,
        grid=(t // tm,),
        in_specs=[
            pl.BlockSpec((tm, D_MODEL), row),
            pl.BlockSpec((1, D_MODEL), fixed),
            pl.BlockSpec((D_MODEL, OFF_G), fixed, pipeline_mode=pl.Buffered(1)),
            pl.BlockSpec((1, HEAD_DIM), fixed),
            pl.BlockSpec((1, HEAD_DIM), fixed),
        ],
        out_specs=[
            pl.BlockSpec((tm, F_W), row),
            pl.BlockSpec((tm, C_W), row),
            pl.BlockSpec((tm, A_W), row),
            pl.BlockSpec((tm, A_W), row),
            pl.BlockSpec((tm, A_W), row),
        ],
        out_shape=[
            jax.ShapeDtypeStruct((t, F_W), BF16),
            jax.ShapeDtypeStruct((t, C_W), F32),
            jax.ShapeDtypeStruct((t, A_W), BF16),
            jax.ShapeDtypeStruct((t, A_W), BF16),
            jax.ShapeDtypeStruct((t, A_W), BF16),
        ],
        compiler_params=pltpu.CompilerParams(
            dimension_semantics=("arbitrary",), vmem_limit_bytes=_vmem_limit(vmem)),
        name="in_proj",
    )(x, g1, w_in, qg, kg)


def _dft_factors(s):
    s1 = 256 if s >= 8192 else 128
    return s1, s // s1


@functools.lru_cache(maxsize=None)
def _dft_tables(s):
    s1, s2 = _dft_factors(s)
    n1 = np.arange(s1)
    ang1 = 2.0 * np.pi * np.outer(n1, n1) / s1
    f1 = np.concatenate([np.cos(ang1), -np.sin(ang1)], axis=0) / np.sqrt(s1)
    k1 = np.arange(s1)[:, None, None]
    k2 = np.arange(s2)[None, :, None]
    n2 = np.arange(s2)[None, None, :]
    ang2 = 2.0 * np.pi * ((n2 * (k1 + s1 * k2)) % s) / s
    gr = np.cos(ang2) / np.sqrt(s2)
    gi = -np.sin(ang2) / np.sqrt(s2)
    g2 = np.concatenate([np.concatenate([gr, -gi], axis=2),
                         np.concatenate([gi, gr], axis=2)], axis=1)
    c = np.arange(F_GROUP_DIM)
    ang3 = 2.0 * np.pi * np.outer(c, c) / F_GROUP_DIM
    cs = np.concatenate([np.cos(ang3), np.sin(ang3)], axis=0) / np.sqrt(F_GROUP_DIM)
    as_bf16 = lambda a: jnp.asarray(a.astype(np.float32)).astype(BF16)
    return as_bf16(f1), as_bf16(g2), as_bf16(cs)


def _dft_seq1_kernel(f1_ref, x_ref, a_ref):
    a_ref[0] = jnp.dot(f1_ref[...], x_ref[0], preferred_element_type=F32).astype(BF16)


def _dft_seq2_kernel(g2_ref, cs_ref, a_ref, o_ref, *, kb, s2):
    a2 = jnp.concatenate([a_ref[0, 0], a_ref[0, 1]], axis=1)
    w = jnp.einsum("kmn,knc->kmc", g2_ref[...], a2, preferred_element_type=F32)
    wr = w[:, :s2, :].astype(BF16)
    wi = w[:, s2:, :].astype(BF16)
    for g in range(F_GROUPS):
        gs = slice(g * F_GROUP_DIM, (g + 1) * F_GROUP_DIM)
        lhs = jnp.concatenate([wr[:, :, gs], wi[:, :, gs]], axis=-1).reshape(kb * s2, 2 * F_GROUP_DIM)
        og = jnp.dot(lhs, cs_ref[...], preferred_element_type=F32).astype(BF16)
        og = og.reshape(kb, s2, F_GROUP_DIM)
        for k in range(kb):
            o_ref[0, :, k * F_W + g * F_GROUP_DIM:k * F_W + (g + 1) * F_GROUP_DIM] = og[k]


def _fourier_mix(u, b, s):
    s1, s2 = _dft_factors(s)
    f1, g2, cs = _dft_tables(s)
    nb = min(s2, 8)
    x = u.reshape(b, s1, s2 * F_W)
    a = pl.pallas_call(
        _dft_seq1_kernel,
        grid=(b, s2 // nb),
        in_specs=[
            pl.BlockSpec((2 * s1, s1), lambda i, j: (0, 0)),
            pl.BlockSpec((1, s1, nb * F_W), lambda i, j: (i, 0, j)),
        ],
        out_specs=pl.BlockSpec((1, 2 * s1, nb * F_W), lambda i, j: (i, 0, j)),
        out_shape=jax.ShapeDtypeStruct((b, 2 * s1, s2 * F_W), BF16),
        compiler_params=pltpu.CompilerParams(
            dimension_semantics=("arbitrary", "arbitrary"),
            vmem_limit_bytes=_vmem_limit(2 * 3 * s1 * nb * F_W * 2 + 2 * s1 * nb * F_W * 4)),
        name="dft_seq1",
    )(f1, x)
    a = a.reshape(b, 2, s1, s2, F_W)
    kb = 16
    out = pl.pallas_call(
        functools.partial(_dft_seq2_kernel, kb=kb, s2=s2),
        grid=(b, s1 // kb),
        in_specs=[
            pl.BlockSpec((kb, 2 * s2, 2 * s2), lambda i, j: (j, 0, 0)),
            pl.BlockSpec((2 * F_GROUP_DIM, F_GROUP_DIM), lambda i, j: (0, 0)),
            pl.BlockSpec((1, 2, kb, s2, F_W), lambda i, j: (i, 0, j, 0, 0)),
        ],
        out_specs=pl.BlockSpec((1, s2, kb * F_W), lambda i, j: (i, 0, j)),
        out_shape=jax.ShapeDtypeStruct((b, s2, s1 * F_W), BF16),
        compiler_params=pltpu.CompilerParams(
            dimension_semantics=("arbitrary", "arbitrary"),
            vmem_limit_bytes=_vmem_limit(2 * 3 * kb * s2 * F_W * 2 + 4 * kb * s2 * F_W * 4)),
        name="dft_seq2",
    )(g2, cs, a)
    return out.reshape(b * s, F_W)


def _conv_kernel(prev_ref, cur_ref, next_ref, dw_ref, db_ref, lg_ref, lb_ref, o_ref, pad_ref):
    i = pl.program_id(1)
    last = pl.num_programs(1) - 1
    pad_ref[0:CONV_HALO, :] = jnp.where(i > 0, prev_ref[0, 0], 0.0)
    pad_ref[CONV_HALO:CONV_HALO + CONV_ROWS, :] = cur_ref[0]
    pad_ref[CONV_HALO + CONV_ROWS:, :] = jnp.where(i < last, next_ref[0, 0], 0.0)
    first = CONV_HALO - CONV_PAD
    for c in range(CONV_ROWS // CONV_CHUNK):
        r0 = c * CONV_CHUNK
        acc = jnp.broadcast_to(db_ref[...], (CONV_CHUNK, C_W))
        for t in range(CONV_K):
            acc = acc + dw_ref[t] * pad_ref[pl.ds(r0 + first + t, CONV_CHUNK), :]
        mu = jnp.mean(acc, axis=-1, keepdims=True)
        xc = acc - mu
        var = jnp.mean(xc * xc, axis=-1, keepdims=True)
        y = (xc * lax.rsqrt(var + EPS)) * lg_ref[...] + lb_ref[...]
        o_ref[0, r0:r0 + CONV_CHUNK, :] = (y * jax.nn.sigmoid(y)).astype(BF16)


def _conv_module(glu, b, s, dw, db, lg, lb):
    nblk = s // CONV_ROWS
    per = CONV_ROWS // CONV_HALO
    x3 = glu.reshape(b, s, C_W)
    x4 = glu.reshape(b, s // CONV_HALO, CONV_HALO, C_W)
    dwb = jnp.broadcast_to(dw[:, None, :], (CONV_K, CONV_CHUNK, C_W))
    vec = lambda i, j: (0, 0)
    out = pl.pallas_call(
        _conv_kernel,
        grid=(b, nblk),
        in_specs=[
            pl.BlockSpec((1, 1, CONV_HALO, C_W), lambda i, j: (i, jnp.maximum(j * per - 1, 0), 0, 0)),
            pl.BlockSpec((1, CONV_ROWS, C_W), lambda i, j: (i, j, 0)),
            pl.BlockSpec((1, 1, CONV_HALO, C_W),
                         lambda i, j: (i, jnp.minimum((j + 1) * per, s // CONV_HALO - 1), 0, 0)),
            pl.BlockSpec((CONV_K, CONV_CHUNK, C_W), lambda i, j: (0, 0, 0)),
            pl.BlockSpec((1, C_W), vec),
            pl.BlockSpec((1, C_W), vec),
            pl.BlockSpec((1, C_W), vec),
        ],
        out_specs=pl.BlockSpec((1, CONV_ROWS, C_W), lambda i, j: (i, j, 0)),
        out_shape=jax.ShapeDtypeStruct((b, s, C_W), BF16),
        scratch_shapes=[pltpu.VMEM((CONV_ROWS + 2 * CONV_HALO, C_W), F32)],
        compiler_params=pltpu.CompilerParams(
            dimension_semantics=("arbitrary", "arbitrary"),
            vmem_limit_bytes=_vmem_limit(4 * CONV_ROWS * C_W * 4 + 2 * CONV_K * CONV_CHUNK * C_W * 4)),
        name="conv",
    )(x4, x3, x4, dwb, db, lg, lb)
    return out.reshape(b * s, C_W)


@functools.lru_cache(maxsize=None)
def _natten_index_tables():
    n_key_rows = K_GROUPS * Q_ROWS
    n_dr, n_dc = 2 * WIN_R - 1, 2 * WIN_C - 1
    i = np.arange(Q_ROWS)[:, None]
    rr = np.arange(n_key_rows)[None, :]
    lo = (np.zeros_like(i), i, np.full_like(i, n_key_rows - WIN_R))
    dr_off = (WIN_R - 1, WIN_R - 1 - Q_ROWS, WIN_R - 1 - 2 * Q_ROWS)
    row_sel = np.zeros((3, Q_ROWS, n_key_rows, n_dr), np.float32)
    for p in range(3):
        row_ok = (rr >= lo[p]) & (rr < lo[p] + WIN_R)
        dr = rr - i + dr_off[p]
        for a in range(Q_ROWS):
            for r in range(n_key_rows):
                if row_ok[a, r]:
                    row_sel[p, a, r, dr[a, r]] = 1.0
    qc = np.arange(GRID_W)[:, None]
    kc = np.arange(GRID_W)[None, :]
    start = np.clip(qc - WIN_C // 2, 0, GRID_W - WIN_C)
    col_ok = (kc >= start) & (kc < start + WIN_C)
    dc = kc - qc + (WIN_C - 1)
    col_sel = np.zeros((n_dc, GRID_W, GRID_W), np.float32)
    for a in range(GRID_W):
        for c in range(GRID_W):
            if col_ok[a, c]:
                col_sel[dc[a, c], a, c] = 1.0
    ok = (row_sel.sum(-1) > 0)[:, :, None, :, None] & col_ok[None, None, :, None, :]
    return row_sel, col_sel, ok.reshape(3, Q_TOK, n_key_rows * GRID_W)


def _natten_bias_table(rpb):
    row_sel, col_sel, ok = _natten_index_tables()
    tab = jnp.einsum("pird,hdc,cqk->phiqrk", row_sel, rpb, col_sel, precision=lax.Precision.HIGHEST)
    tab = tab.reshape(3, N_HEADS, Q_TOK, K_GROUPS * Q_TOK)
    return jnp.where(ok[:, None], tab, NEG_INF)


def _natten_kernel(q_ref, k0_ref, k1_ref, k2_ref, v0_ref, v1_ref, v2_ref, tab_ref, o_ref):
    k_refs = (k0_ref, k1_ref, k2_ref)
    v_refs = (v0_ref, v1_ref, v2_ref)
    for hd in range(N_HEADS):
        hs = slice(hd * HEAD_DIM, (hd + 1) * HEAD_DIM)
        qh = q_ref[:, hs]
        sc = []
        for g in range(K_GROUPS):
            s = lax.dot_general(qh, k_refs[g][:, hs], (((1,), (1,)), ((), ())),
                                preferred_element_type=F32)
            t = tab_ref[0, hd, :, g * Q_TOK:(g + 1) * Q_TOK]
            sc.append(jnp.where(t > 0.5 * NEG_INF, s + t, NEG_INF))
        m = sc[0].max(axis=-1, keepdims=True)
        for g in range(1, K_GROUPS):
            m = jnp.maximum(m, sc[g].max(axis=-1, keepdims=True))
        denom = None
        o = None
        for g in range(K_GROUPS):
            p = jnp.exp(sc[g] - m)
            ps = p.sum(axis=-1, keepdims=True)
            pv = jnp.dot(p.astype(BF16), v_refs[g][:, hs], preferred_element_type=F32)
            denom = ps if denom is None else denom + ps
            o = pv if o is None else o + pv
        o_ref[:, hs] = (o / denom).astype(BF16)


def _natten(q, k, v, tab, b, s):
    ng = s // Q_TOK
    assert ng >= K_GROUPS

    def q_map(i, j):
        return (i * ng + j, 0)

    def kv_map(off):
        return lambda i, j: (i * ng + jnp.clip(j - 1, 0, ng - K_GROUPS) + off, 0)

    def tab_map(i, j):
        return (jnp.where(j == 0, 0, jnp.where(j == ng - 1, 2, 1)), 0, 0, 0)

    blk = (Q_TOK, A_W)
    vmem = 2 * (2 + 2 * K_GROUPS) * Q_TOK * A_W * 2 + 2 * N_HEADS * Q_TOK * K_GROUPS * Q_TOK * 4
    return pl.pallas_call(
        _natten_kernel,
        grid=(b, ng),
        in_specs=[pl.BlockSpec(blk, q_map)]
        + [pl.BlockSpec(blk, kv_map(g)) for g in range(K_GROUPS)]
        + [pl.BlockSpec(blk, kv_map(g)) for g in range(K_GROUPS)]
        + [pl.BlockSpec((1, N_HEADS, Q_TOK, K_GROUPS * Q_TOK), tab_map)],
        out_specs=pl.BlockSpec(blk, q_map),
        out_shape=jax.ShapeDtypeStruct((b * s, A_W), BF16),
        compiler_params=pltpu.CompilerParams(
            dimension_semantics=("arbitrary", "arbitrary"), vmem_limit_bytes=_vmem_limit(vmem)),
        name="natten",
    )(q, k, k, k, v, v, v, tab)


def _merge_kernel(x_ref, g1_ref, fm_ref, uc_ref, at_ref, wg0_ref, wg1_ref, wg2_ref,
                  bg0_ref, bg1_ref, bg2_ref, wf_ref, wc_ref, wa_ref, wo_ref, o_ref, h_ref):
    @pl.when(pl.program_id(1) == 0)
    def _():
        x = x_ref[...]
        h_ref[...] = _rms_norm_bf16(x, g1_ref[...])
        o_ref[...] = x

    h = h_ref[...]

    def gate(w_ref, b_ref):
        return jax.nn.sigmoid(jnp.dot(h, w_ref[...], preferred_element_type=F32) + b_ref[...])

    merged = gate(wg0_ref, bg0_ref) * jnp.dot(fm_ref[...], wf_ref[...], preferred_element_type=F32)
    merged += gate(wg1_ref, bg1_ref) * jnp.dot(uc_ref[...], wc_ref[...], preferred_element_type=F32)
    merged += gate(wg2_ref, bg2_ref) * jnp.dot(at_ref[...], wa_ref[...], preferred_element_type=F32)
    o_ref[...] += jnp.dot(merged.astype(BF16), wo_ref[...], preferred_element_type=F32)


def _merge(x, g1, fm, uc, at, w_in, b_gate, w_f, w_c, w_a, w_o):
    t = x.shape[0]
    tm, tn = TOKEN_TILE, MERGE_COLS
    nj = D_MODEL // tn
    gate_blk = OFF_G // tn
    row = lambda i, j: (i, 0)
    col = lambda i, j: (0, j)
    gate_specs = [pl.BlockSpec((D_MODEL, tn), lambda i, j, br=br: (0, gate_blk + br * nj + j))
                  for br in range(N_BRANCH)]
    bias_specs = [pl.BlockSpec((1, tn), lambda i, j, br=br: (0, br * nj + j)) for br in range(N_BRANCH)]
    vmem = (4 * tm * D_MODEL * 4 + 2 * tm * (F_W + C_W + A_W) * 2 + tm * D_MODEL * 2
            + 2 * (3 * D_MODEL + F_W + C_W + A_W + D_MODEL) * tn * 2 + 6 * tm * tn * 4)
    return pl.pallas_call(
        _merge_kernel,
        grid=(t // tm, nj),
        in_specs=[
            pl.BlockSpec((tm, D_MODEL), row),
            pl.BlockSpec((1, D_MODEL), lambda i, j: (0, 0)),
            pl.BlockSpec((tm, F_W), row),
            pl.BlockSpec((tm, C_W), row),
            pl.BlockSpec((tm, A_W), row),
            *gate_specs,
            *bias_specs,
            pl.BlockSpec((F_W, tn), col),
            pl.BlockSpec((C_W, tn), col),
            pl.BlockSpec((A_W, tn), col),
            pl.BlockSpec((tn, D_MODEL), lambda i, j: (j, 0)),
        ],
        out_specs=pl.BlockSpec((tm, D_MODEL), row),
        out_shape=jax.ShapeDtypeStruct((t, D_MODEL), F32),
        scratch_shapes=[pltpu.VMEM((tm, D_MODEL), BF16)],
        compiler_params=pltpu.CompilerParams(
            dimension_semantics=("arbitrary", "arbitrary"), vmem_limit_bytes=_vmem_limit(vmem)),
        name="merge",
    )(x, g1, fm, uc, at, w_in, w_in, w_in, b_gate, b_gate, b_gate, w_f, w_c, w_a, w_o)


def _mlp_kernel(x_ref, g2_ref, w1_ref, w2_ref, o_ref, h_ref):
    @pl.when(pl.program_id(1) == 0)
    def _():
        x = x_ref[...]
        h_ref[...] = _rms_norm_bf16(x, g2_ref[...])
        o_ref[...] = x

    a = jnp.maximum(jnp.dot(h_ref[...], w1_ref[...], preferred_element_type=F32), 0.0)
    o_ref[...] += jnp.dot((a * a).astype(BF16), w2_ref[...], preferred_element_type=F32)


def _mlp(x, g2, w1, w2):
    t = x.shape[0]
    tm, tf = TOKEN_TILE, MLP_COLS
    row = lambda i, j: (i, 0)
    vmem = 4 * tm * D_MODEL * 4 + tm * D_MODEL * 2 + 2 * 2 * D_MODEL * tf * 2 + 2 * tm * tf * 4
    return pl.pallas_call(
        _mlp_kernel,
        grid=(t // tm, D_FF // tf),
        in_specs=[
            pl.BlockSpec((tm, D_MODEL), row),
            pl.BlockSpec((1, D_MODEL), lambda i, j: (0, 0)),
            pl.BlockSpec((D_MODEL, tf), lambda i, j: (0, j)),
            pl.BlockSpec((tf, D_MODEL), lambda i, j: (j, 0)),
        ],
        out_specs=pl.BlockSpec((tm, D_MODEL), row),
        out_shape=jax.ShapeDtypeStruct((t, D_MODEL), F32),
        scratch_shapes=[pltpu.VMEM((tm, D_MODEL), BF16)],
        compiler_params=pltpu.CompilerParams(
            dimension_semantics=("arbitrary", "arbitrary"), vmem_limit_bytes=_vmem_limit(vmem)),
        name="mlp",
    )(x, g2, w1, w2)


def _layer(x, b, s, p):
    f, glu, q, k, v = _in_proj(x, p["n1"], p["w_in"], p["qg"], p["kg"])
    fm = _fourier_mix(f, b, s)
    uc = _conv_module(glu, b, s, p["dw"], p["db"], p["ln_g"], p["ln_b"])
    at = _natten(q, k, v, p["tab"], b, s)
    x = _merge(x, p["n1"], fm, uc, at, p["w_in"], p["b_g"], p["w_f"], p["w_c"], p["w_a"], p["w_o"])
    return _mlp(x, p["n2"], p["w1"], p["w2"])


def kernel(x_prompt, x_sample, norm1_g, w_in, b_gate, w_fourier, conv_dw, conv_db, conv_ln_g, conv_ln_b,
           w_conv_out, q_norm_g, k_norm_g, rpb, w_attn_out, w_out, norm2_g, w_mlp_in, w_mlp_out):
    depth = w_in.shape[0]
    layers = []
    for l in range(depth):
        layers.append(dict(
            n1=norm1_g[l][None, :], n2=norm2_g[l][None, :],
            w_in=w_in[l].astype(BF16), b_g=b_gate[l][None, :],
            w_f=w_fourier[l].astype(BF16), w_c=w_conv_out[l].astype(BF16),
            w_a=w_attn_out[l].astype(BF16), w_o=w_out[l].astype(BF16),
            w1=w_mlp_in[l].astype(BF16), w2=w_mlp_out[l].astype(BF16),
            dw=conv_dw[l], db=conv_db[l][None, :], ln_g=conv_ln_g[l][None, :], ln_b=conv_ln_b[l][None, :],
            qg=q_norm_g[l][None, :], kg=k_norm_g[l][None, :],
            tab=_natten_bias_table(rpb[l]),
        ))

    def trunk(x):
        b, s, d = x.shape
        y = x.reshape(b * s, d)
        for p in layers:
            y = _layer(y, b, s, p)
        return y.reshape(b, s, d)

    return (trunk(x_prompt), trunk(x_sample))
```

```python
import functools
import math

import numpy as np
import jax
import jax.numpy as jnp
from jax import lax
from jax.experimental import pallas as pl
from jax.experimental.pallas import tpu as pltpu

D_MODEL = 2048
GRID_W = 64
F_GROUPS = 4
F_GROUP_DIM = 128
F_W = F_GROUPS * F_GROUP_DIM
C_W = 512
CONV_K = 31
CONV_PAD = CONV_K // 2
N_HEADS = 8
HEAD_DIM = 128
A_W = N_HEADS * HEAD_DIM
WIN_R = 8
WIN_C = 16
N_BRANCH = 3
OFF_F = 0
OFF_CA = OFF_F + F_W
OFF_CG = OFF_CA + C_W
OFF_Q = OFF_CG + C_W
OFF_K = OFF_Q + A_W
OFF_V = OFF_K + A_W
OFF_G = OFF_V + A_W
IN_COLS = OFF_G + N_BRANCH * D_MODEL
D_FF = 4 * D_MODEL
EPS = 1e-6
NEG_INF = -1e30

BF16 = jnp.bfloat16
F32 = jnp.float32

V7X_VMEM_BYTES = 64 * 1024 * 1024
V7X_LANES = 128
V7X_SUBLANES = 8

TOKEN_TILE = 512
MERGE_COLS = 512
MLP_COLS = 1024
CONV_ROWS = 256
CONV_HALO = 16
CONV_CHUNK = 32
Q_ROWS = 4
K_GROUPS = 3
Q_TOK = Q_ROWS * GRID_W


def _vmem_limit(nbytes):
    return int(min(nbytes * 5 // 4 + (8 << 20), V7X_VMEM_BYTES - (4 << 20)))


def _rms_norm_bf16(x, gain):
    ms = jnp.mean(x * x, axis=-1, keepdims=True)
    return ((x * lax.rsqrt(ms + EPS)) * gain).astype(BF16)


def _in_proj_kernel(x_ref, g1_ref, w_ref, qg_ref, kg_ref, f_ref, glu_ref, q_ref, kt_ref, v_ref):
    h = _rms_norm_bf16(x_ref[...], g1_ref[...])

    def proj(c0, n):
        return jnp.dot(h, w_ref[:, c0:c0 + n], preferred_element_type=F32)

    f_ref[...] = proj(OFF_F, F_W).astype(BF16)
    a = proj(OFF_CA, C_W)
    g = proj(OFF_CG, C_W)
    glu_ref[...] = a * jax.nn.sigmoid(g)

    def head_norm(z, gain, scale, store):
        for hd in range(N_HEADS):
            zh = z[:, hd * HEAD_DIM:(hd + 1) * HEAD_DIM]
            ms = jnp.mean(zh * zh, axis=-1, keepdims=True)
            y = (zh * lax.rsqrt(ms + EPS)) * gain
            if scale != 1.0:
                y = y * scale
            store(hd, y)

    def store_q(hd, y):
        q_ref[:, hd * HEAD_DIM:(hd + 1) * HEAD_DIM] = y.astype(BF16)

    def store_k_transposed(hd, y):
        kt_ref[hd] = jnp.transpose(y).astype(BF16)

    head_norm(proj(OFF_Q, A_W), qg_ref[...], 1.0 / math.sqrt(HEAD_DIM), store_q)
    head_norm(proj(OFF_K, A_W), kg_ref[...], 1.0, store_k_transposed)
    v_ref[...] = proj(OFF_V, A_W).astype(BF16)


def _in_proj(x, g1, w_in, qg, kg):
    t = x.shape[0]
    tm = TOKEN_TILE
    row = lambda i: (i, 0)
    fixed = lambda i: (0, 0)
    vmem = (2 * tm * D_MODEL * 4 + D_MODEL * OFF_G * 2
            + 2 * tm * (F_W * 2 + C_W * 4 + 3 * A_W * 2) + 2 * tm * A_W * 4)
    return pl.pallas_call(
        _in_proj_kernel,
        grid=(t // tm,),
        in_specs=[
            pl.BlockSpec((tm, D_MODEL), row),
            pl.BlockSpec((1, D_MODEL), fixed),
            pl.BlockSpec((D_MODEL, OFF_G), fixed, pipeline_mode=pl.Buffered(1)),
            pl.BlockSpec((1, HEAD_DIM), fixed),
            pl.BlockSpec((1, HEAD_DIM), fixed),
        ],
        out_specs=[
            pl.BlockSpec((tm, F_W), row),
            pl.BlockSpec((tm, C_W), row),
            pl.BlockSpec((tm, A_W), row),
            pl.BlockSpec((N_HEADS, HEAD_DIM, tm), lambda i: (0, 0, i)),
            pl.BlockSpec((tm, A_W), row),
        ],
        out_shape=[
            jax.ShapeDtypeStruct((t, F_W), BF16),
            jax.ShapeDtypeStruct((t, C_W), F32),
            jax.ShapeDtypeStruct((t, A_W), BF16),
            jax.ShapeDtypeStruct((N_HEADS, HEAD_DIM, t), BF16),
            jax.ShapeDtypeStruct((t, A_W), BF16),
        ],
        compiler_params=pltpu.CompilerParams(
            dimension_semantics=("arbitrary",), vmem_limit_bytes=_vmem_limit(vmem)),
        name="in_proj",
    )(x, g1, w_in, qg, kg)


def _dft_factors(s):
    s1 = 256 if s >= 8192 else 128
    return s1, s // s1


@functools.lru_cache(maxsize=None)
def _dft_tables(s):
    s1, s2 = _dft_factors(s)
    n1 = np.arange(s1)
    ang1 = 2.0 * np.pi * np.outer(n1, n1) / s1
    f1 = np.concatenate([np.cos(ang1), -np.sin(ang1)], axis=0) / np.sqrt(s1)
    k1 = np.arange(s1)[:, None, None]
    k2 = np.arange(s2)[None, :, None]
    n2 = np.arange(s2)[None, None, :]
    ang2 = 2.0 * np.pi * ((n2 * (k1 + s1 * k2)) % s) / s
    gr = np.cos(ang2) / np.sqrt(s2)
    gi = -np.sin(ang2) / np.sqrt(s2)
    g2 = np.concatenate([np.concatenate([gr, -gi], axis=2),
                         np.concatenate([gi, gr], axis=2)], axis=1)
    c = np.arange(F_GROUP_DIM)
    ang3 = 2.0 * np.pi * np.outer(c, c) / F_GROUP_DIM
    cs = np.concatenate([np.cos(ang3), np.sin(ang3)], axis=0) / np.sqrt(F_GROUP_DIM)
    as_bf16 = lambda a: jnp.asarray(a.astype(np.float32)).astype(BF16)
    return as_bf16(f1), as_bf16(g2), as_bf16(cs)


def _dft_seq1_kernel(f1_ref, x_ref, a_ref):
    a_ref[0] = jnp.dot(f1_ref[...], x_ref[0], preferred_element_type=F32).astype(BF16)


def _dft_seq2_kernel(g2_ref, cs_ref, a_ref, o_ref, *, kb, s2):
    a2 = jnp.concatenate([a_ref[0, 0], a_ref[0, 1]], axis=1)
    w = jnp.einsum("kmn,knc->kmc", g2_ref[...], a2, preferred_element_type=F32)
    wr = w[:, :s2, :].astype(BF16)
    wi = w[:, s2:, :].astype(BF16)
    for g in range(F_GROUPS):
        gs = slice(g * F_GROUP_DIM, (g + 1) * F_GROUP_DIM)
        lhs = jnp.concatenate([wr[:, :, gs], wi[:, :, gs]], axis=-1).reshape(kb * s2, 2 * F_GROUP_DIM)
        og = jnp.dot(lhs, cs_ref[...], preferred_element_type=F32).astype(BF16)
        og = og.reshape(kb, s2, F_GROUP_DIM)
        for k in range(kb):
            o_ref[0, :, k * F_W + g * F_GROUP_DIM:k * F_W + (g + 1) * F_GROUP_DIM] = og[k]


def _fourier_mix(u, b, s):
    s1, s2 = _dft_factors(s)
    f1, g2, cs = _dft_tables(s)
    nb = min(s2, 8)
    x = u.reshape(b, s1, s2 * F_W)
    a = pl.pallas_call(
        _dft_seq1_kernel,
        grid=(b, s2 // nb),
        in_specs=[
            pl.BlockSpec((2 * s1, s1), lambda i, j: (0, 0)),
            pl.BlockSpec((1, s1, nb * F_W), lambda i, j: (i, 0, j)),
        ],
        out_specs=pl.BlockSpec((1, 2 * s1, nb * F_W), lambda i, j: (i, 0, j)),
        out_shape=jax.ShapeDtypeStruct((b, 2 * s1, s2 * F_W), BF16),
        compiler_params=pltpu.CompilerParams(
            dimension_semantics=("arbitrary", "arbitrary"),
            vmem_limit_bytes=_vmem_limit(2 * 3 * s1 * nb * F_W * 2 + 2 * s1 * nb * F_W * 4)),
        name="dft_seq1",
    )(f1, x)
    a = a.reshape(b, 2, s1, s2, F_W)
    kb = 16
    out = pl.pallas_call(
        functools.partial(_dft_seq2_kernel, kb=kb, s2=s2),
        grid=(b, s1 // kb),
        in_specs=[
            pl.BlockSpec((kb, 2 * s2, 2 * s2), lambda i, j: (j, 0, 0)),
            pl.BlockSpec((2 * F_GROUP_DIM, F_GROUP_DIM), lambda i, j: (0, 0)),
            pl.BlockSpec((1, 2, kb, s2, F_W), lambda i, j: (i, 0, j, 0, 0)),
        ],
        out_specs=pl.BlockSpec((1, s2, kb * F_W), lambda i, j: (i, 0, j)),
        out_shape=jax.ShapeDtypeStruct((b, s2, s1 * F_W), BF16),
        compiler_params=pltpu.CompilerParams(
            dimension_semantics=("arbitrary", "arbitrary"),
            vmem_limit_bytes=_vmem_limit(2 * 3 * kb * s2 * F_W * 2 + 4 * kb * s2 * F_W * 4)),
        name="dft_seq2",
    )(g2, cs, a)
    return out.reshape(b * s, F_W)


def _conv_kernel(prev_ref, cur_ref, next_ref, dw_ref, db_ref, lg_ref, lb_ref, o_ref, pad_ref, conv_ref):
    i = pl.program_id(1)
    last = pl.num_programs(1) - 1
    prev = jnp.where(i > 0, prev_ref[0, 0], 0.0)
    nxt = jnp.where(i < last, next_ref[0, 0], 0.0)
    n_lt = C_W // V7X_LANES
    lane_tiles = [slice(c * V7X_LANES, (c + 1) * V7X_LANES) for c in range(n_lt)]
    for c, ls in enumerate(lane_tiles):
        pad_ref[c, 0:CONV_HALO, :] = prev[:, ls]
        pad_ref[c, CONV_HALO:CONV_HALO + CONV_ROWS, :] = cur_ref[0, :, ls]
        pad_ref[c, CONV_HALO + CONV_ROWS:, :] = nxt[:, ls]
    first = CONV_HALO - CONV_PAD
    n_sub = CONV_CHUNK // V7X_SUBLANES
    for c, ls in enumerate(lane_tiles):
        w = [dw_ref[t, :, ls] for t in range(CONV_K)]
        bias = jnp.broadcast_to(db_ref[:, ls], (V7X_SUBLANES, V7X_LANES))

        def chunk_body(ch, carry, c=c, w=w, bias=bias):
            r0 = pl.multiple_of(ch * CONV_CHUNK, CONV_CHUNK)
            acc = [bias for _ in range(n_sub)]
            for t in range(CONV_K):
                for k in range(n_sub):
                    row = r0 + (k * V7X_SUBLANES + first + t)
                    acc[k] = acc[k] + w[t] * pad_ref[c, pl.ds(row, V7X_SUBLANES), :]
            conv_ref[c, pl.ds(r0, CONV_CHUNK), :] = jnp.concatenate(acc, axis=0)
            return carry

        lax.fori_loop(0, CONV_ROWS // CONV_CHUNK, chunk_body, 0)

    for ch in range(CONV_ROWS // CONV_CHUNK):
        r0 = ch * CONV_CHUNK
        accs = [conv_ref[c, r0:r0 + CONV_CHUNK, :] for c in range(n_lt)]
        mu = sum(accs).sum(axis=-1, keepdims=True) * (1.0 / C_W)
        xcs = [a - mu for a in accs]
        var = sum(x * x for x in xcs).sum(axis=-1, keepdims=True) * (1.0 / C_W)
        rstd = lax.rsqrt(var + EPS)
        for c, ls in enumerate(lane_tiles):
            y = (xcs[c] * rstd) * lg_ref[:, ls] + lb_ref[:, ls]
            o_ref[0, r0:r0 + CONV_CHUNK, ls] = (y * jax.nn.sigmoid(y)).astype(BF16)


def _conv_module(glu, b, s, dw, db, lg, lb):
    nblk = s // CONV_ROWS
    per = CONV_ROWS // CONV_HALO
    x3 = glu.reshape(b, s, C_W)
    x4 = glu.reshape(b, s // CONV_HALO, CONV_HALO, C_W)
    dwb = jnp.broadcast_to(dw[:, None, :], (CONV_K, V7X_SUBLANES, C_W))
    vec = lambda i, j: (0, 0)
    out = pl.pallas_call(
        _conv_kernel,
        grid=(b, nblk),
        in_specs=[
            pl.BlockSpec((1, 1, CONV_HALO, C_W), lambda i, j: (i, jnp.maximum(j * per - 1, 0), 0, 0)),
            pl.BlockSpec((1, CONV_ROWS, C_W), lambda i, j: (i, j, 0)),
            pl.BlockSpec((1, 1, CONV_HALO, C_W),
                         lambda i, j: (i, jnp.minimum((j + 1) * per, s // CONV_HALO - 1), 0, 0)),
            pl.BlockSpec((CONV_K, V7X_SUBLANES, C_W), lambda i, j: (0, 0, 0)),
            pl.BlockSpec((1, C_W), vec),
            pl.BlockSpec((1, C_W), vec),
            pl.BlockSpec((1, C_W), vec),
        ],
        out_specs=pl.BlockSpec((1, CONV_ROWS, C_W), lambda i, j: (i, j, 0)),
        out_shape=jax.ShapeDtypeStruct((b, s, C_W), BF16),
        scratch_shapes=[pltpu.VMEM((C_W // V7X_LANES, CONV_ROWS + 2 * CONV_HALO, V7X_LANES), F32),
                        pltpu.VMEM((C_W // V7X_LANES, CONV_ROWS, V7X_LANES), F32)],
        compiler_params=pltpu.CompilerParams(
            dimension_semantics=("arbitrary", "arbitrary"),
            vmem_limit_bytes=_vmem_limit(4 * CONV_ROWS * C_W * 4 + 2 * CONV_K * V7X_SUBLANES * C_W * 4)),
        name="conv",
    )(x4, x3, x4, dwb, db, lg, lb)
    return out.reshape(b * s, C_W)


@functools.lru_cache(maxsize=None)
def _natten_index_tables():
    n_key_rows = K_GROUPS * Q_ROWS
    n_dr, n_dc = 2 * WIN_R - 1, 2 * WIN_C - 1
    i = np.arange(Q_ROWS)[:, None]
    rr = np.arange(n_key_rows)[None, :]
    lo = (np.zeros_like(i), i, np.full_like(i, n_key_rows - WIN_R))
    dr_off = (WIN_R - 1, WIN_R - 1 - Q_ROWS, WIN_R - 1 - 2 * Q_ROWS)
    row_sel = np.zeros((3, Q_ROWS, n_key_rows, n_dr), np.float32)
    for p in range(3):
        row_ok = (rr >= lo[p]) & (rr < lo[p] + WIN_R)
        dr = rr - i + dr_off[p]
        for a in range(Q_ROWS):
            for r in range(n_key_rows):
                if row_ok[a, r]:
                    row_sel[p, a, r, dr[a, r]] = 1.0
    qc = np.arange(GRID_W)[:, None]
    kc = np.arange(GRID_W)[None, :]
    start = np.clip(qc - WIN_C // 2, 0, GRID_W - WIN_C)
    col_ok = (kc >= start) & (kc < start + WIN_C)
    dc = kc - qc + (WIN_C - 1)
    col_sel = np.zeros((n_dc, GRID_W, GRID_W), np.float32)
    for a in range(GRID_W):
        for c in range(GRID_W):
            if col_ok[a, c]:
                col_sel[dc[a, c], a, c] = 1.0
    ok = (row_sel.sum(-1) > 0)[:, :, None, :, None] & col_ok[None, None, :, None, :]
    return row_sel, col_sel, ok.reshape(3, Q_TOK, n_key_rows * GRID_W)


def _natten_bias_table(rpb):
    row_sel, col_sel, ok = _natten_index_tables()
    tab = jnp.einsum("pird,hdc,cqk->phiqrk", row_sel, rpb, col_sel, precision=lax.Precision.HIGHEST)
    tab = tab.reshape(3, N_HEADS, Q_TOK, K_GROUPS * Q_TOK)
    return jnp.where(ok[:, None], tab, NEG_INF)


def _natten_kernel(q_ref, kt0_ref, kt1_ref, kt2_ref, v0_ref, v1_ref, v2_ref, tab_ref, o_ref):
    kt_refs = (kt0_ref, kt1_ref, kt2_ref)
    v_refs = (v0_ref, v1_ref, v2_ref)
    def scores(hd):
        qh = q_ref[:, hd * HEAD_DIM:(hd + 1) * HEAD_DIM]
        return [jnp.dot(qh, kt_refs[g][hd], preferred_element_type=F32)
                + tab_ref[0, hd, :, g * Q_TOK:(g + 1) * Q_TOK] for g in range(K_GROUPS)]

    def fold_lanes(parts, op):
        acc = parts[0]
        for part in parts[1:]:
            acc = op(acc, part)
        tiles = [acc[:, c * V7X_LANES:(c + 1) * V7X_LANES] for c in range(acc.shape[1] // V7X_LANES)]
        out = tiles[0]
        for tile in tiles[1:]:
            out = op(out, tile)
        return out

    sc_next = scores(0)
    for hd in range(N_HEADS):
        hs = slice(hd * HEAD_DIM, (hd + 1) * HEAD_DIM)
        sc = sc_next
        if hd + 1 < N_HEADS:
            sc_next = scores(hd + 1)
        m = fold_lanes(sc, jnp.maximum).max(axis=-1, keepdims=True)
        p = [jnp.exp(s - m) for s in sc]
        denom = fold_lanes(p, jnp.add).sum(axis=-1, keepdims=True)
        o = jnp.dot(p[0].astype(BF16), v_refs[0][:, hs], preferred_element_type=F32)
        for g in range(1, K_GROUPS):
            o += jnp.dot(p[g].astype(BF16), v_refs[g][:, hs], preferred_element_type=F32)
        o_ref[:, hs] = (o * (1.0 / denom)).astype(BF16)


def _natten(q, kt, v, tab, b, s):
    ng = s // Q_TOK
    assert ng >= K_GROUPS

    def q_map(i, j):
        return (i * ng + j, 0)

    def window_block(i, j, off):
        return i * ng + jnp.clip(j - 1, 0, ng - K_GROUPS) + off

    def tab_map(i, j):
        return (jnp.where(j == 0, 0, jnp.where(j == ng - 1, 2, 1)), 0, 0, 0)

    blk = (Q_TOK, A_W)
    vmem = 2 * (2 + 2 * K_GROUPS) * Q_TOK * A_W * 2 + 2 * N_HEADS * Q_TOK * K_GROUPS * Q_TOK * 4
    return pl.pallas_call(
        _natten_kernel,
        grid=(b, ng),
        in_specs=[pl.BlockSpec(blk, q_map)]
        + [pl.BlockSpec((N_HEADS, HEAD_DIM, Q_TOK), lambda i, j, g=g: (0, 0, window_block(i, j, g)))
           for g in range(K_GROUPS)]
        + [pl.BlockSpec(blk, lambda i, j, g=g: (window_block(i, j, g), 0)) for g in range(K_GROUPS)]
        + [pl.BlockSpec((1, N_HEADS, Q_TOK, K_GROUPS * Q_TOK), tab_map)],
        out_specs=pl.BlockSpec(blk, q_map),
        out_shape=jax.ShapeDtypeStruct((b * s, A_W), BF16),
        compiler_params=pltpu.CompilerParams(
            dimension_semantics=("arbitrary", "arbitrary"), vmem_limit_bytes=_vmem_limit(vmem)),
        name="natten",
    )(q, kt, kt, kt, v, v, v, tab)


def _merge_kernel(x_ref, g1_ref, fm_ref, uc_ref, at_ref, wg0_ref, wg1_ref, wg2_ref,
                  bg0_ref, bg1_ref, bg2_ref, wf_ref, wc_ref, wa_ref, wo_ref, o_ref, h_ref):
    @pl.when(pl.program_id(1) == 0)
    def _():
        x = x_ref[...]
        h_ref[...] = _rms_norm_bf16(x, g1_ref[...])
        o_ref[...] = x

    h = h_ref[...]

    def gate(w_ref, b_ref):
        return jax.nn.sigmoid(jnp.dot(h, w_ref[...], preferred_element_type=F32) + b_ref[...])

    merged = gate(wg0_ref, bg0_ref) * jnp.dot(fm_ref[...], wf_ref[...], preferred_element_type=F32)
    merged += gate(wg1_ref, bg1_ref) * jnp.dot(uc_ref[...], wc_ref[...], preferred_element_type=F32)
    merged += gate(wg2_ref, bg2_ref) * jnp.dot(at_ref[...], wa_ref[...], preferred_element_type=F32)
    o_ref[...] += jnp.dot(merged.astype(BF16), wo_ref[...], preferred_element_type=F32)


def _merge(x, g1, fm, uc, at, w_in, b_gate, w_f, w_c, w_a, w_o):
    t = x.shape[0]
    tm, tn = TOKEN_TILE, MERGE_COLS
    nj = D_MODEL // tn
    gate_blk = OFF_G // tn
    row = lambda i, j: (i, 0)
    col = lambda i, j: (0, j)
    gate_specs = [pl.BlockSpec((D_MODEL, tn), lambda i, j, br=br: (0, gate_blk + br * nj + j))
                  for br in range(N_BRANCH)]
    bias_specs = [pl.BlockSpec((1, tn), lambda i, j, br=br: (0, br * nj + j)) for br in range(N_BRANCH)]
    vmem = (4 * tm * D_MODEL * 4 + 2 * tm * (F_W + C_W + A_W) * 2 + tm * D_MODEL * 2
            + 2 * (3 * D_MODEL + F_W + C_W + A_W + D_MODEL) * tn * 2 + 6 * tm * tn * 4)
    return pl.pallas_call(
        _merge_kernel,
        grid=(t // tm, nj),
        in_specs=[
            pl.BlockSpec((tm, D_MODEL), row),
            pl.BlockSpec((1, D_MODEL), lambda i, j: (0, 0)),
            pl.BlockSpec((tm, F_W), row),
            pl.BlockSpec((tm, C_W), row),
            pl.BlockSpec((tm, A_W), row),
            *gate_specs,
            *bias_specs,
            pl.BlockSpec((F_W, tn), col),
            pl.BlockSpec((C_W, tn), col),
            pl.BlockSpec((A_W, tn), col),
            pl.BlockSpec((tn, D_MODEL), lambda i, j: (j, 0)),
        ],
        out_specs=pl.BlockSpec((tm, D_MODEL), row),
        out_shape=jax.ShapeDtypeStruct((t, D_MODEL), F32),
        scratch_shapes=[pltpu.VMEM((tm, D_MODEL), BF16)],
        compiler_params=pltpu.CompilerParams(
            dimension_semantics=("arbitrary", "arbitrary"), vmem_limit_bytes=_vmem_limit(vmem)),
        name="merge",
    )(x, g1, fm, uc, at, w_in, w_in, w_in, b_gate, b_gate, b_gate, w_f, w_c, w_a, w_o)


def _mlp_kernel(x_ref, g2_ref, w1_ref, w2_ref, o_ref, h_ref):
    @pl.when(pl.program_id(1) == 0)
    def _():
        x = x_ref[...]
        h_ref[...] = _rms_norm_bf16(x, g2_ref[...])
        o_ref[...] = x

    a = jnp.maximum(jnp.dot(h_ref[...], w1_ref[...], preferred_element_type=F32), 0.0)
    o_ref[...] += jnp.dot((a * a).astype(BF16), w2_ref[...], preferred_element_type=F32)


def _mlp(x, g2, w1, w2):
    t = x.shape[0]
    tm, tf = TOKEN_TILE, MLP_COLS
    row = lambda i, j: (i, 0)
    vmem = 4 * tm * D_MODEL * 4 + tm * D_MODEL * 2 + 2 * 2 * D_MODEL * tf * 2 + 2 * tm * tf * 4
    return pl.pallas_call(
        _mlp_kernel,
        grid=(t // tm, D_FF // tf),
        in_specs=[
            pl.BlockSpec((tm, D_MODEL), row),
            pl.BlockSpec((1, D_MODEL), lambda i, j: (0, 0)),
            pl.BlockSpec((D_MODEL, tf), lambda i, j: (0, j)),
            pl.BlockSpec((tf, D_MODEL), lambda i, j: (j, 0)),
        ],
        out_specs=pl.BlockSpec((tm, D_MODEL), row),
        out_shape=jax.ShapeDtypeStruct((t, D_MODEL), F32),
        scratch_shapes=[pltpu.VMEM((tm, D_MODEL), BF16)],
        compiler_params=pltpu.CompilerParams(
            dimension_semantics=("arbitrary", "arbitrary"), vmem_limit_bytes=_vmem_limit(vmem)),
        name="mlp",
    )(x, g2, w1, w2)


def _layer(x, b, s, p):
    f, glu, q, kt, v = _in_proj(x, p["n1"], p["w_in"], p["qg"], p["kg"])
    fm = _fourier_mix(f, b, s)
    uc = _conv_module(glu, b, s, p["dw"], p["db"], p["ln_g"], p["ln_b"])
    at = _natten(q, kt, v, p["tab"], b, s)
    x = _merge(x, p["n1"], fm, uc, at, p["w_in"], p["b_g"], p["w_f"], p["w_c"], p["w_a"], p["w_o"])
    return _mlp(x, p["n2"], p["w1"], p["w2"])


def kernel(x_prompt, x_sample, norm1_g, w_in, b_gate, w_fourier, conv_dw, conv_db, conv_ln_g, conv_ln_b,
           w_conv_out, q_norm_g, k_norm_g, rpb, w_attn_out, w_out, norm2_g, w_mlp_in, w_mlp_out):
    depth = w_in.shape[0]
    layers = []
    for l in range(depth):
        layers.append(dict(
            n1=norm1_g[l][None, :], n2=norm2_g[l][None, :],
            w_in=w_in[l].astype(BF16), b_g=b_gate[l][None, :],
            w_f=w_fourier[l].astype(BF16), w_c=w_conv_out[l].astype(BF16),
            w_a=w_attn_out[l].astype(BF16), w_o=w_out[l].astype(BF16),
            w1=w_mlp_in[l].astype(BF16), w2=w_mlp_out[l].astype(BF16),
            dw=conv_dw[l], db=conv_db[l][None, :], ln_g=conv_ln_g[l][None, :], ln_b=conv_ln_b[l][None, :],
            qg=q_norm_g[l][None, :], kg=k_norm_g[l][None, :],
            tab=_natten_bias_table(rpb[l]),
        ))

    def trunk(x):
        b, s, d = x.shape
        y = x.reshape(b * s, d)
        for p in layers:
            y = _layer(y, b, s, p)
        return y.reshape(b, s, d)

    return (trunk(x_prompt), trunk(x_sample))
```

```python
import functools
import math

import numpy as np
import jax
import jax.numpy as jnp
from jax import lax
from jax.experimental import pallas as pl
from jax.experimental.pallas import tpu as pltpu

D_MODEL = 2048
GRID_W = 64
F_GROUPS = 4
F_GROUP_DIM = 128
F_W = F_GROUPS * F_GROUP_DIM
C_W = 512
CONV_K = 31
CONV_PAD = CONV_K // 2
N_HEADS = 8
HEAD_DIM = 128
A_W = N_HEADS * HEAD_DIM
WIN_R = 8
WIN_C = 16
N_BRANCH = 3
OFF_F = 0
OFF_CA = OFF_F + F_W
OFF_CG = OFF_CA + C_W
OFF_Q = OFF_CG + C_W
OFF_K = OFF_Q + A_W
OFF_V = OFF_K + A_W
OFF_G = OFF_V + A_W
IN_COLS = OFF_G + N_BRANCH * D_MODEL
D_FF = 4 * D_MODEL
EPS = 1e-6
NEG_INF = -1e30

BF16 = jnp.bfloat16
F32 = jnp.float32

V7X_VMEM_BYTES = 64 * 1024 * 1024
V7X_LANES = 128
V7X_SUBLANES = 8

TOKEN_TILE = 512
MERGE_COLS = 512
MLP_COLS = 1024
CONV_ROWS = 256
CONV_HALO = 16
CONV_CHUNK = 32
DFT_K1_BLOCK = 16
Q_ROWS = 4
K_GROUPS = 3
Q_TOK = Q_ROWS * GRID_W


def _vmem_limit(nbytes):
    return int(min(nbytes * 5 // 4 + (8 << 20), V7X_VMEM_BYTES - (4 << 20)))


def _rms_norm_bf16(x, gain):
    ms = jnp.mean(x * x, axis=-1, keepdims=True)
    return ((x * lax.rsqrt(ms + EPS)) * gain).astype(BF16)


LOG2E = math.log2(math.e)


def _in_proj_kernel(x_ref, g1_ref, w_ref, qg_ref, kg_ref, f_ref, glu_ref, q_ref, kt_ref, v_ref, zf_ref, *, s2):
    h = _rms_norm_bf16(x_ref[...], g1_ref[...])

    def proj(c0, n):
        return jnp.dot(h, w_ref[:, c0:c0 + n], preferred_element_type=F32)

    zf = proj(OFF_F, F_W)
    n_lt = F_W // V7X_LANES
    for c in range(n_lt):
        zf_ref[c] = zf[:, c * V7X_LANES:(c + 1) * V7X_LANES]
    rows = zf.shape[0] // s2
    for n2 in range(s2):
        for c in range(n_lt):
            piece = zf_ref[c, pl.ds(n2, rows, stride=s2), :]
            lane0 = n2 * F_W + c * V7X_LANES
            f_ref[:, lane0:lane0 + V7X_LANES] = piece.astype(BF16)

    a = proj(OFF_CA, C_W)
    g = proj(OFF_CG, C_W)
    glu_ref[...] = a * jax.nn.sigmoid(g)

    def head_norm(z, gain, scale, store):
        for hd in range(N_HEADS):
            zh = z[:, hd * HEAD_DIM:(hd + 1) * HEAD_DIM]
            ms = jnp.mean(zh * zh, axis=-1, keepdims=True)
            y = (zh * lax.rsqrt(ms + EPS)) * gain
            if scale != 1.0:
                y = y * scale
            store(hd, y)

    def store_q(hd, y):
        q_ref[:, hd * HEAD_DIM:(hd + 1) * HEAD_DIM] = y.astype(BF16)

    def store_k_transposed(hd, y):
        kt_ref[hd] = jnp.transpose(y).astype(BF16)

    head_norm(proj(OFF_Q, A_W), qg_ref[...], LOG2E / math.sqrt(HEAD_DIM), store_q)
    head_norm(proj(OFF_K, A_W), kg_ref[...], 1.0, store_k_transposed)
    v_ref[...] = proj(OFF_V, A_W).astype(BF16)


def _in_proj(x, layer, g1, w_in, qg, kg, s2):
    t = x.shape[0]
    tm = TOKEN_TILE
    row = lambda i: (i, 0)
    fixed = lambda i: (0, 0)
    vmem = (2 * tm * D_MODEL * 4 + D_MODEL * OFF_G * 2
            + 2 * tm * (F_W * 2 + C_W * 4 + 3 * A_W * 2) + 2 * tm * A_W * 4 + tm * F_W * 4)
    return pl.pallas_call(
        functools.partial(_in_proj_kernel, s2=s2),
        grid=(t // tm,),
        in_specs=[
            pl.BlockSpec((tm, D_MODEL), row),
            pl.BlockSpec((1, D_MODEL), fixed),
            pl.BlockSpec((None, D_MODEL, OFF_G), lambda i: (layer, 0, 0), pipeline_mode=pl.Buffered(1)),
            pl.BlockSpec((1, HEAD_DIM), fixed),
            pl.BlockSpec((1, HEAD_DIM), fixed),
        ],
        out_specs=[
            pl.BlockSpec((tm // s2, s2 * F_W), row),
            pl.BlockSpec((tm, C_W), row),
            pl.BlockSpec((tm, A_W), row),
            pl.BlockSpec((N_HEADS, HEAD_DIM, tm), lambda i: (0, 0, i)),
            pl.BlockSpec((tm, A_W), row),
        ],
        out_shape=[
            jax.ShapeDtypeStruct((t // s2, s2 * F_W), BF16),
            jax.ShapeDtypeStruct((t, C_W), F32),
            jax.ShapeDtypeStruct((t, A_W), BF16),
            jax.ShapeDtypeStruct((N_HEADS, HEAD_DIM, t), BF16),
            jax.ShapeDtypeStruct((t, A_W), BF16),
        ],
        scratch_shapes=[pltpu.VMEM((F_W // V7X_LANES, tm, V7X_LANES), F32)],
        compiler_params=pltpu.CompilerParams(
            dimension_semantics=("arbitrary",), vmem_limit_bytes=_vmem_limit(vmem)),
        name="in_proj",
    )(x, g1, w_in, qg, kg)


def _dft_factors(s):
    s1 = 256 if s >= 8192 else 128
    return s1, s // s1


@functools.lru_cache(maxsize=None)
def _dft_tables(s):
    s1, s2 = _dft_factors(s)
    n1 = np.arange(s1)
    ang1 = 2.0 * np.pi * np.outer(n1, n1) / s1
    f1 = np.concatenate([np.cos(ang1), -np.sin(ang1)], axis=0) / np.sqrt(s1)
    k1 = np.arange(s1)[:, None, None]
    k2 = np.arange(s2)[None, :, None]
    n2 = np.arange(s2)[None, None, :]
    ang2 = 2.0 * np.pi * ((n2 * (k1 + s1 * k2)) % s) / s
    gr = np.cos(ang2) / np.sqrt(s2)
    gi = -np.sin(ang2) / np.sqrt(s2)
    g2 = np.concatenate([np.concatenate([gr, -gi], axis=2),
                         np.concatenate([gi, gr], axis=2)], axis=1)
    c = np.arange(F_GROUP_DIM)
    ang3 = 2.0 * np.pi * np.outer(c, c) / F_GROUP_DIM
    cs = np.concatenate([np.cos(ang3), np.sin(ang3)], axis=0) / np.sqrt(F_GROUP_DIM)
    kb = DFT_K1_BLOCK
    perm = np.zeros((kb * s2, kb * s2), np.float32)
    for a in range(kb):
        for r in range(s2):
            perm[r * kb + a, a * s2 + r] = 1.0
    as_bf16 = lambda a: jnp.asarray(a.astype(np.float32)).astype(BF16)
    return as_bf16(f1), as_bf16(g2), as_bf16(cs), as_bf16(perm)


def _dft_seq1_kernel(f1_ref, x_ref, a_ref):
    a_ref[0] = jnp.dot(f1_ref[...], x_ref[0], preferred_element_type=F32).astype(BF16)


def _dft_seq2_kernel(g2_ref, cs_ref, perm_ref, a_ref, o_ref, *, kb, s2):
    a2 = jnp.concatenate([a_ref[0, 0], a_ref[0, 1]], axis=1)
    w = jnp.einsum("kmn,knc->kmc", g2_ref[...], a2, preferred_element_type=F32)
    wr = w[:, :s2, :].astype(BF16)
    wi = w[:, s2:, :].astype(BF16)
    groups = []
    for g in range(F_GROUPS):
        gs = slice(g * F_GROUP_DIM, (g + 1) * F_GROUP_DIM)
        lhs = jnp.concatenate([wr[:, :, gs], wi[:, :, gs]], axis=-1).reshape(kb * s2, 2 * F_GROUP_DIM)
        groups.append(jnp.dot(lhs, cs_ref[...], preferred_element_type=F32).astype(BF16))
    og = jnp.concatenate(groups, axis=-1)
    out = jnp.dot(perm_ref[...], og, preferred_element_type=F32).astype(BF16)
    o_ref[0] = out.reshape(s2, kb, F_W)


def _fourier_mix(x3, b, s):
    s1, s2 = _dft_factors(s)
    f1, g2, cs, perm = _dft_tables(s)
    nb = min(s2, 8)
    x = x3.reshape(b, s1, s2 * F_W)
    a = pl.pallas_call(
        _dft_seq1_kernel,
        grid=(b, s2 // nb),
        in_specs=[
            pl.BlockSpec((2 * s1, s1), lambda i, j: (0, 0)),
            pl.BlockSpec((1, s1, nb * F_W), lambda i, j: (i, 0, j)),
        ],
        out_specs=pl.BlockSpec((1, 2 * s1, nb * F_W), lambda i, j: (i, 0, j)),
        out_shape=jax.ShapeDtypeStruct((b, 2 * s1, s2 * F_W), BF16),
        compiler_params=pltpu.CompilerParams(
            dimension_semantics=("arbitrary", "arbitrary"),
            vmem_limit_bytes=_vmem_limit(2 * 3 * s1 * nb * F_W * 2 + 2 * s1 * nb * F_W * 4)),
        name="dft_seq1",
    )(f1, x)
    a = a.reshape(b, 2, s1, s2, F_W)
    kb = DFT_K1_BLOCK
    out = pl.pallas_call(
        functools.partial(_dft_seq2_kernel, kb=kb, s2=s2),
        grid=(b, s1 // kb),
        in_specs=[
            pl.BlockSpec((kb, 2 * s2, 2 * s2), lambda i, j: (j, 0, 0)),
            pl.BlockSpec((2 * F_GROUP_DIM, F_GROUP_DIM), lambda i, j: (0, 0)),
            pl.BlockSpec((kb * s2, kb * s2), lambda i, j: (0, 0)),
            pl.BlockSpec((1, 2, kb, s2, F_W), lambda i, j: (i, 0, j, 0, 0)),
        ],
        out_specs=pl.BlockSpec((1, s2, kb, F_W), lambda i, j: (i, 0, j, 0)),
        out_shape=jax.ShapeDtypeStruct((b, s2, s1, F_W), BF16),
        compiler_params=pltpu.CompilerParams(
            dimension_semantics=("arbitrary", "arbitrary"),
            vmem_limit_bytes=_vmem_limit(2 * 3 * kb * s2 * F_W * 2 + 4 * kb * s2 * F_W * 4
                                         + 2 * (kb * s2) ** 2 * 2)),
        name="dft_seq2",
    )(g2, cs, perm, a)
    return out.reshape(b * s, F_W)


def _conv_kernel(prev_ref, cur_ref, next_ref, dw_ref, db_ref, lg_ref, lb_ref, o_ref, pad_ref, conv_ref):
    i = pl.program_id(1)
    last = pl.num_programs(1) - 1
    prev = jnp.where(i > 0, prev_ref[0, 0], 0.0)
    nxt = jnp.where(i < last, next_ref[0, 0], 0.0)
    n_lt = C_W // V7X_LANES
    lane_tiles = [slice(c * V7X_LANES, (c + 1) * V7X_LANES) for c in range(n_lt)]
    for c, ls in enumerate(lane_tiles):
        pad_ref[c, 0:CONV_HALO, :] = prev[:, ls]
        pad_ref[c, CONV_HALO:CONV_HALO + CONV_ROWS, :] = cur_ref[0, :, ls]
        pad_ref[c, CONV_HALO + CONV_ROWS:, :] = nxt[:, ls]
    first = CONV_HALO - CONV_PAD
    n_sub = CONV_CHUNK // V7X_SUBLANES
    for c, ls in enumerate(lane_tiles):
        w = [dw_ref[t, :, ls] for t in range(CONV_K)]
        bias = jnp.broadcast_to(db_ref[:, ls], (V7X_SUBLANES, V7X_LANES))

        def chunk_body(ch, carry, c=c, w=w, bias=bias):
            r0 = pl.multiple_of(ch * CONV_CHUNK, CONV_CHUNK)
            acc = [bias for _ in range(n_sub)]
            for t in range(CONV_K):
                for k in range(n_sub):
                    row = r0 + (k * V7X_SUBLANES + first + t)
                    acc[k] = acc[k] + w[t] * pad_ref[c, pl.ds(row, V7X_SUBLANES), :]
            conv_ref[c, pl.ds(r0, CONV_CHUNK), :] = jnp.concatenate(acc, axis=0)
            return carry

        lax.fori_loop(0, CONV_ROWS // CONV_CHUNK, chunk_body, 0)

    for ch in range(CONV_ROWS // CONV_CHUNK):
        r0 = ch * CONV_CHUNK
        accs = [conv_ref[c, r0:r0 + CONV_CHUNK, :] for c in range(n_lt)]
        mu = sum(accs).sum(axis=-1, keepdims=True) * (1.0 / C_W)
        xcs = [a - mu for a in accs]
        var = sum(x * x for x in xcs).sum(axis=-1, keepdims=True) * (1.0 / C_W)
        rstd = lax.rsqrt(var + EPS)
        for c, ls in enumerate(lane_tiles):
            y = (xcs[c] * rstd) * lg_ref[:, ls] + lb_ref[:, ls]
            o_ref[0, r0:r0 + CONV_CHUNK, ls] = (y * jax.nn.sigmoid(y)).astype(BF16)


def _conv_module(glu, b, s, dw, db, lg, lb):
    nblk = s // CONV_ROWS
    per = CONV_ROWS // CONV_HALO
    x3 = glu.reshape(b, s, C_W)
    x4 = glu.reshape(b, s // CONV_HALO, CONV_HALO, C_W)
    dwb = jnp.broadcast_to(dw[:, None, :], (CONV_K, V7X_SUBLANES, C_W))
    vec = lambda i, j: (0, 0)
    out = pl.pallas_call(
        _conv_kernel,
        grid=(b, nblk),
        in_specs=[
            pl.BlockSpec((1, 1, CONV_HALO, C_W), lambda i, j: (i, jnp.maximum(j * per - 1, 0), 0, 0)),
            pl.BlockSpec((1, CONV_ROWS, C_W), lambda i, j: (i, j, 0)),
            pl.BlockSpec((1, 1, CONV_HALO, C_W),
                         lambda i, j: (i, jnp.minimum((j + 1) * per, s // CONV_HALO - 1), 0, 0)),
            pl.BlockSpec((CONV_K, V7X_SUBLANES, C_W), lambda i, j: (0, 0, 0)),
            pl.BlockSpec((1, C_W), vec),
            pl.BlockSpec((1, C_W), vec),
            pl.BlockSpec((1, C_W), vec),
        ],
        out_specs=pl.BlockSpec((1, CONV_ROWS, C_W), lambda i, j: (i, j, 0)),
        out_shape=jax.ShapeDtypeStruct((b, s, C_W), BF16),
        scratch_shapes=[pltpu.VMEM((C_W // V7X_LANES, CONV_ROWS + 2 * CONV_HALO, V7X_LANES), F32),
                        pltpu.VMEM((C_W // V7X_LANES, CONV_ROWS, V7X_LANES), F32)],
        compiler_params=pltpu.CompilerParams(
            dimension_semantics=("arbitrary", "arbitrary"),
            vmem_limit_bytes=_vmem_limit(4 * CONV_ROWS * C_W * 4 + 2 * CONV_K * V7X_SUBLANES * C_W * 4)),
        name="conv",
    )(x4, x3, x4, dwb, db, lg, lb)
    return out.reshape(b * s, C_W)


@functools.lru_cache(maxsize=None)
def _natten_index_tables():
    n_key_rows = K_GROUPS * Q_ROWS
    n_dr, n_dc = 2 * WIN_R - 1, 2 * WIN_C - 1
    i = np.arange(Q_ROWS)[:, None]
    rr = np.arange(n_key_rows)[None, :]
    lo = (np.zeros_like(i), i, np.full_like(i, n_key_rows - WIN_R))
    dr_off = (WIN_R - 1, WIN_R - 1 - Q_ROWS, WIN_R - 1 - 2 * Q_ROWS)
    row_sel = np.zeros((3, Q_ROWS, n_key_rows, n_dr), np.float32)
    for p in range(3):
        row_ok = (rr >= lo[p]) & (rr < lo[p] + WIN_R)
        dr = rr - i + dr_off[p]
        for a in range(Q_ROWS):
            for r in range(n_key_rows):
                if row_ok[a, r]:
                    row_sel[p, a, r, dr[a, r]] = 1.0
    qc = np.arange(GRID_W)[:, None]
    kc = np.arange(GRID_W)[None, :]
    start = np.clip(qc - WIN_C // 2, 0, GRID_W - WIN_C)
    col_ok = (kc >= start) & (kc < start + WIN_C)
    dc = kc - qc + (WIN_C - 1)
    col_sel = np.zeros((n_dc, GRID_W, GRID_W), np.float32)
    for a in range(GRID_W):
        for c in range(GRID_W):
            if col_ok[a, c]:
                col_sel[dc[a, c], a, c] = 1.0
    ok = (row_sel.sum(-1) > 0)[:, :, None, :, None] & col_ok[None, None, :, None, :]
    return row_sel, col_sel, ok.reshape(3, Q_TOK, n_key_rows * GRID_W)


def _natten_bias_table(rpb):
    row_sel, col_sel, ok = _natten_index_tables()
    tab = jnp.einsum("pird,hdc,cqk->phiqrk", row_sel, rpb, col_sel, precision=lax.Precision.HIGHEST)
    tab = tab.reshape(3, N_HEADS, Q_TOK, K_GROUPS * Q_TOK)
    return jnp.where(ok[:, None], tab * LOG2E, NEG_INF)


def _natten_kernel(q_ref, kt0_ref, kt1_ref, kt2_ref, v0_ref, v1_ref, v2_ref, tab_ref, o_ref):
    kt_refs = (kt0_ref, kt1_ref, kt2_ref)
    v_refs = (v0_ref, v1_ref, v2_ref)
    def scores(hd):
        qh = q_ref[:, hd * HEAD_DIM:(hd + 1) * HEAD_DIM]
        return [jnp.dot(qh, kt_refs[g][hd], preferred_element_type=F32)
                + tab_ref[0, hd, :, g * Q_TOK:(g + 1) * Q_TOK] for g in range(K_GROUPS)]

    def fold_lanes(parts, op):
        acc = parts[0]
        for part in parts[1:]:
            acc = op(acc, part)
        tiles = [acc[:, c * V7X_LANES:(c + 1) * V7X_LANES] for c in range(acc.shape[1] // V7X_LANES)]
        out = tiles[0]
        for tile in tiles[1:]:
            out = op(out, tile)
        return out

    sc_next = scores(0)
    for hd in range(N_HEADS):
        hs = slice(hd * HEAD_DIM, (hd + 1) * HEAD_DIM)
        sc = sc_next
        if hd + 1 < N_HEADS:
            sc_next = scores(hd + 1)
        m = fold_lanes(sc, jnp.maximum).max(axis=-1, keepdims=True)
        p = [jnp.exp2(s - m) for s in sc]
        denom = fold_lanes(p, jnp.add).sum(axis=-1, keepdims=True)
        o = jnp.dot(p[0].astype(BF16), v_refs[0][:, hs], preferred_element_type=F32)
        for g in range(1, K_GROUPS):
            o += jnp.dot(p[g].astype(BF16), v_refs[g][:, hs], preferred_element_type=F32)
        o_ref[:, hs] = (o * (1.0 / denom)).astype(BF16)


def _natten(q, kt, v, tab, b, s):
    ng = s // Q_TOK
    assert ng >= K_GROUPS

    def q_map(i, j):
        return (i * ng + j, 0)

    def window_block(i, j, off):
        return i * ng + jnp.clip(j - 1, 0, ng - K_GROUPS) + off

    def tab_map(i, j):
        return (jnp.where(j == 0, 0, jnp.where(j == ng - 1, 2, 1)), 0, 0, 0)

    blk = (Q_TOK, A_W)
    vmem = 2 * (2 + 2 * K_GROUPS) * Q_TOK * A_W * 2 + 2 * N_HEADS * Q_TOK * K_GROUPS * Q_TOK * 4
    return pl.pallas_call(
        _natten_kernel,
        grid=(b, ng),
        in_specs=[pl.BlockSpec(blk, q_map)]
        + [pl.BlockSpec((N_HEADS, HEAD_DIM, Q_TOK), lambda i, j, g=g: (0, 0, window_block(i, j, g)))
           for g in range(K_GROUPS)]
        + [pl.BlockSpec(blk, lambda i, j, g=g: (window_block(i, j, g), 0)) for g in range(K_GROUPS)]
        + [pl.BlockSpec((1, N_HEADS, Q_TOK, K_GROUPS * Q_TOK), tab_map)],
        out_specs=pl.BlockSpec(blk, q_map),
        out_shape=jax.ShapeDtypeStruct((b * s, A_W), BF16),
        compiler_params=pltpu.CompilerParams(
            dimension_semantics=("arbitrary", "arbitrary"), vmem_limit_bytes=_vmem_limit(vmem)),
        name="natten",
    )(q, kt, kt, kt, v, v, v, tab)


def _merge_kernel(x_ref, g1_ref, fm_ref, uc_ref, at_ref, wg0_ref, wg1_ref, wg2_ref,
                  bg0_ref, bg1_ref, bg2_ref, wf_ref, wc_ref, wa_ref, wo_ref, o_ref, h_ref):
    @pl.when(pl.program_id(1) == 0)
    def _():
        x = x_ref[...]
        h_ref[...] = _rms_norm_bf16(x, g1_ref[...])
        o_ref[...] = x

    h = h_ref[...]

    def gate(w_ref, b_ref):
        return jax.nn.sigmoid(jnp.dot(h, w_ref[...], preferred_element_type=F32) + b_ref[...])

    merged = gate(wg0_ref, bg0_ref) * jnp.dot(fm_ref[...], wf_ref[...], preferred_element_type=F32)
    merged += gate(wg1_ref, bg1_ref) * jnp.dot(uc_ref[...], wc_ref[...], preferred_element_type=F32)
    merged += gate(wg2_ref, bg2_ref) * jnp.dot(at_ref[...], wa_ref[...], preferred_element_type=F32)
    o_ref[...] += jnp.dot(merged.astype(BF16), wo_ref[...], preferred_element_type=F32)


def _merge(x, layer, g1, fm, uc, at, w_in, b_gate, w_f, w_c, w_a, w_o):
    t = x.shape[0]
    tm, tn = TOKEN_TILE, MERGE_COLS
    nj = D_MODEL // tn
    gate_blk = OFF_G // tn
    row = lambda i, j: (i, 0)
    col = lambda i, j: (layer, 0, j)
    gate_specs = [pl.BlockSpec((None, D_MODEL, tn), lambda i, j, br=br: (layer, 0, gate_blk + br * nj + j))
                  for br in range(N_BRANCH)]
    bias_specs = [pl.BlockSpec((1, tn), lambda i, j, br=br: (0, br * nj + j)) for br in range(N_BRANCH)]
    vmem = (4 * tm * D_MODEL * 4 + 2 * tm * (F_W + C_W + A_W) * 2 + tm * D_MODEL * 2
            + 2 * (3 * D_MODEL + F_W + C_W + A_W + D_MODEL) * tn * 2 + 6 * tm * tn * 4)
    return pl.pallas_call(
        _merge_kernel,
        grid=(t // tm, nj),
        in_specs=[
            pl.BlockSpec((tm, D_MODEL), row),
            pl.BlockSpec((1, D_MODEL), lambda i, j: (0, 0)),
            pl.BlockSpec((tm, F_W), row),
            pl.BlockSpec((tm, C_W), row),
            pl.BlockSpec((tm, A_W), row),
            *gate_specs,
            *bias_specs,
            pl.BlockSpec((None, F_W, tn), col),
            pl.BlockSpec((None, C_W, tn), col),
            pl.BlockSpec((None, A_W, tn), col),
            pl.BlockSpec((None, tn, D_MODEL), lambda i, j: (layer, j, 0)),
        ],
        out_specs=pl.BlockSpec((tm, D_MODEL), row),
        out_shape=jax.ShapeDtypeStruct((t, D_MODEL), F32),
        scratch_shapes=[pltpu.VMEM((tm, D_MODEL), BF16)],
        compiler_params=pltpu.CompilerParams(
            dimension_semantics=("arbitrary", "arbitrary"), vmem_limit_bytes=_vmem_limit(vmem)),
        name="merge",
    )(x, g1, fm, uc, at, w_in, w_in, w_in, b_gate, b_gate, b_gate, w_f, w_c, w_a, w_o)


def _mlp_kernel(x_ref, g2_ref, w1_ref, w2_ref, o_ref, h_ref):
    @pl.when(pl.program_id(1) == 0)
    def _():
        x = x_ref[...]
        h_ref[...] = _rms_norm_bf16(x, g2_ref[...])
        o_ref[...] = x

    a = jnp.maximum(jnp.dot(h_ref[...], w1_ref[...], preferred_element_type=F32), 0.0)
    o_ref[...] += jnp.dot((a * a).astype(BF16), w2_ref[...], preferred_element_type=F32)


def _mlp(x, layer, g2, w1, w2):
    t = x.shape[0]
    tm, tf = TOKEN_TILE, MLP_COLS
    row = lambda i, j: (i, 0)
    vmem = 4 * tm * D_MODEL * 4 + tm * D_MODEL * 2 + 2 * 2 * D_MODEL * tf * 2 + 2 * tm * tf * 4
    return pl.pallas_call(
        _mlp_kernel,
        grid=(t // tm, D_FF // tf),
        in_specs=[
            pl.BlockSpec((tm, D_MODEL), row),
            pl.BlockSpec((1, D_MODEL), lambda i, j: (0, 0)),
            pl.BlockSpec((None, D_MODEL, tf), lambda i, j: (layer, 0, j)),
            pl.BlockSpec((None, tf, D_MODEL), lambda i, j: (layer, j, 0)),
        ],
        out_specs=pl.BlockSpec((tm, D_MODEL), row),
        out_shape=jax.ShapeDtypeStruct((t, D_MODEL), F32),
        scratch_shapes=[pltpu.VMEM((tm, D_MODEL), BF16)],
        compiler_params=pltpu.CompilerParams(
            dimension_semantics=("arbitrary", "arbitrary"), vmem_limit_bytes=_vmem_limit(vmem)),
        name="mlp",
    )(x, g2, w1, w2)


def _layer(x, b, s, l, w, p):
    _, s2 = _dft_factors(s)
    x3, glu, q, kt, v = _in_proj(x, l, p["n1"], w["w_in"], p["qg"], p["kg"], s2)
    fm = _fourier_mix(x3, b, s)
    uc = _conv_module(glu, b, s, p["dw"], p["db"], p["ln_g"], p["ln_b"])
    at = _natten(q, kt, v, p["tab"], b, s)
    x = _merge(x, l, p["n1"], fm, uc, at, w["w_in"], p["b_g"], w["w_f"], w["w_c"], w["w_a"], w["w_o"])
    return _mlp(x, l, p["n2"], w["w1"], w["w2"])


def kernel(x_prompt, x_sample, norm1_g, w_in, b_gate, w_fourier, conv_dw, conv_db, conv_ln_g, conv_ln_b,
           w_conv_out, q_norm_g, k_norm_g, rpb, w_attn_out, w_out, norm2_g, w_mlp_in, w_mlp_out):
    depth = w_in.shape[0]
    w = dict(w_in=w_in.astype(BF16), w_f=w_fourier.astype(BF16), w_c=w_conv_out.astype(BF16),
             w_a=w_attn_out.astype(BF16), w_o=w_out.astype(BF16),
             w1=w_mlp_in.astype(BF16), w2=w_mlp_out.astype(BF16))
    layers = []
    for l in range(depth):
        layers.append(dict(
            n1=norm1_g[l][None, :], n2=norm2_g[l][None, :], b_g=b_gate[l][None, :],
            dw=conv_dw[l], db=conv_db[l][None, :], ln_g=conv_ln_g[l][None, :], ln_b=conv_ln_b[l][None, :],
            qg=q_norm_g[l][None, :], kg=k_norm_g[l][None, :],
            tab=_natten_bias_table(rpb[l]),
        ))

    def trunk(x):
        b, s, d = x.shape
        y = x.reshape(b * s, d)
        for l, p in enumerate(layers):
            y = _layer(y, b, s, l, w, p)
        return y.reshape(b, s, d)

    return (trunk(x_prompt), trunk(x_sample))
```

```python
import functools
import math

import numpy as np
import jax
import jax.numpy as jnp
from jax import lax
from jax.experimental import pallas as pl
from jax.experimental.pallas import tpu as pltpu

D_MODEL = 2048
GRID_W = 64
F_GROUPS = 4
F_GROUP_DIM = 128
F_W = F_GROUPS * F_GROUP_DIM
C_W = 512
CONV_K = 31
CONV_PAD = CONV_K // 2
N_HEADS = 8
HEAD_DIM = 128
A_W = N_HEADS * HEAD_DIM
WIN_R = 8
WIN_C = 16
N_BRANCH = 3
OFF_F = 0
OFF_CA = OFF_F + F_W
OFF_CG = OFF_CA + C_W
OFF_Q = OFF_CG + C_W
OFF_K = OFF_Q + A_W
OFF_V = OFF_K + A_W
OFF_G = OFF_V + A_W
IN_COLS = OFF_G + N_BRANCH * D_MODEL
D_FF = 4 * D_MODEL
EPS = 1e-6
NEG_INF = -1e30

BF16 = jnp.bfloat16
F32 = jnp.float32

V7X_VMEM_BYTES = 64 * 1024 * 1024
V7X_LANES = 128
V7X_SUBLANES = 8

TOKEN_TILE = 512
MERGE_COLS = 512
MLP_TOKEN_TILE = 1024
MLP_COLS = 512
CONV_ROWS = 256
CONV_HALO = 16
CONV_CHUNK = 32
DFT_BLOCK_ROWS = 512
Q_ROWS = 4
K_GROUPS = 3
Q_TOK = Q_ROWS * GRID_W


def _vmem_limit(nbytes):
    return int(min(nbytes * 5 // 4 + (8 << 20), V7X_VMEM_BYTES - (4 << 20)))


def _rms_norm_bf16(x, gain):
    ms = jnp.mean(x * x, axis=-1, keepdims=True)
    return ((x * lax.rsqrt(ms + EPS)) * gain).astype(BF16)


LOG2E = math.log2(math.e)


def _in_proj_kernel(x_ref, g1_ref, w_ref, qg_ref, kg_ref, f_ref, glu_ref, q_ref, kt_ref, v_ref, zf_ref, *, s2):
    h = _rms_norm_bf16(x_ref[...], g1_ref[...])

    def proj(c0, n):
        return jnp.dot(h, w_ref[:, c0:c0 + n], preferred_element_type=F32)

    zf = proj(OFF_F, F_W)
    n_lt = F_W // V7X_LANES
    for c in range(n_lt):
        zf_ref[c] = zf[:, c * V7X_LANES:(c + 1) * V7X_LANES]
    rows = zf.shape[0] // s2
    for n2 in range(s2):
        for c in range(n_lt):
            piece = zf_ref[c, pl.ds(n2, rows, stride=s2), :]
            lane0 = n2 * F_W + c * V7X_LANES
            f_ref[:, lane0:lane0 + V7X_LANES] = piece.astype(BF16)

    a = proj(OFF_CA, C_W)
    g = proj(OFF_CG, C_W)
    glu_ref[...] = a * jax.nn.sigmoid(g)

    def head_norm(z, gain, scale, store):
        for hd in range(N_HEADS):
            zh = z[:, hd * HEAD_DIM:(hd + 1) * HEAD_DIM]
            ms = jnp.mean(zh * zh, axis=-1, keepdims=True)
            y = (zh * lax.rsqrt(ms + EPS)) * gain
            if scale != 1.0:
                y = y * scale
            store(hd, y)

    def store_q(hd, y):
        q_ref[:, hd * HEAD_DIM:(hd + 1) * HEAD_DIM] = y.astype(BF16)

    def store_k_transposed(hd, y):
        yt = jnp.transpose(y).astype(BF16)
        for blk in range(y.shape[0] // Q_TOK):
            kt_ref[blk, hd] = yt[:, blk * Q_TOK:(blk + 1) * Q_TOK]

    head_norm(proj(OFF_Q, A_W), qg_ref[...], LOG2E / math.sqrt(HEAD_DIM), store_q)
    head_norm(proj(OFF_K, A_W), kg_ref[...], 1.0, store_k_transposed)
    v_ref[...] = proj(OFF_V, A_W).astype(BF16)


def _in_proj(x, layer, g1, w_in, qg, kg, s2):
    t = x.shape[0]
    tm = TOKEN_TILE
    row = lambda i: (i, 0)
    fixed = lambda i: (0, 0)
    vmem = (2 * tm * D_MODEL * 4 + D_MODEL * OFF_G * 2
            + 2 * tm * (F_W * 2 + C_W * 4 + 3 * A_W * 2) + 2 * tm * A_W * 4 + tm * F_W * 4)
    return pl.pallas_call(
        functools.partial(_in_proj_kernel, s2=s2),
        grid=(t // tm,),
        in_specs=[
            pl.BlockSpec((tm, D_MODEL), row),
            pl.BlockSpec((1, D_MODEL), fixed),
            pl.BlockSpec((None, D_MODEL, OFF_G), lambda i: (layer, 0, 0), pipeline_mode=pl.Buffered(1)),
            pl.BlockSpec((1, HEAD_DIM), fixed),
            pl.BlockSpec((1, HEAD_DIM), fixed),
        ],
        out_specs=[
            pl.BlockSpec((tm // s2, s2 * F_W), row),
            pl.BlockSpec((tm, C_W), row),
            pl.BlockSpec((tm, A_W), row),
            pl.BlockSpec((tm // Q_TOK, N_HEADS, HEAD_DIM, Q_TOK), lambda i: (i, 0, 0, 0)),
            pl.BlockSpec((tm, A_W), row),
        ],
        out_shape=[
            jax.ShapeDtypeStruct((t // s2, s2 * F_W), BF16),
            jax.ShapeDtypeStruct((t, C_W), F32),
            jax.ShapeDtypeStruct((t, A_W), BF16),
            jax.ShapeDtypeStruct((t // Q_TOK, N_HEADS, HEAD_DIM, Q_TOK), BF16),
            jax.ShapeDtypeStruct((t, A_W), BF16),
        ],
        scratch_shapes=[pltpu.VMEM((F_W // V7X_LANES, tm, V7X_LANES), F32)],
        compiler_params=pltpu.CompilerParams(
            dimension_semantics=("arbitrary",), vmem_limit_bytes=_vmem_limit(vmem)),
        name="in_proj",
    )(x, g1, w_in, qg, kg)


def _dft_factors(s):
    s1 = 256 if s >= 8192 else 128
    return s1, s // s1


@functools.lru_cache(maxsize=None)
def _dft_tables(s):
    s1, s2 = _dft_factors(s)
    n1 = np.arange(s1)
    ang1 = 2.0 * np.pi * np.outer(n1, n1) / s1
    f1 = np.concatenate([np.cos(ang1), -np.sin(ang1)], axis=0) / np.sqrt(s1)
    k1 = np.arange(s1)[:, None, None]
    k2 = np.arange(s2)[None, :, None]
    n2 = np.arange(s2)[None, None, :]
    ang2 = 2.0 * np.pi * ((n2 * (k1 + s1 * k2)) % s) / s
    gr = np.cos(ang2) / np.sqrt(s2)
    gi = -np.sin(ang2) / np.sqrt(s2)
    g2 = np.concatenate([np.concatenate([gr, -gi], axis=2),
                         np.concatenate([gi, gr], axis=2)], axis=1)
    c = np.arange(F_GROUP_DIM)
    ang3 = 2.0 * np.pi * np.outer(c, c) / F_GROUP_DIM
    cs = np.concatenate([np.cos(ang3), np.sin(ang3)], axis=0) / np.sqrt(F_GROUP_DIM)
    kb = DFT_BLOCK_ROWS // s2
    perm = np.zeros((kb * s2, kb * s2), np.float32)
    for a in range(kb):
        for r in range(s2):
            perm[r * kb + a, a * s2 + r] = 1.0
    as_bf16 = lambda a: jnp.asarray(a.astype(np.float32)).astype(BF16)
    return as_bf16(f1), as_bf16(g2), as_bf16(cs), as_bf16(perm)


def _dft_seq1_kernel(f1_ref, x_ref, a_ref):
    a_ref[0] = jnp.dot(f1_ref[...], x_ref[0], preferred_element_type=F32).astype(BF16)


def _dft_seq2_kernel(g2_ref, cs_ref, perm_ref, a_ref, o_ref, *, kb, s2):
    a2 = jnp.concatenate([a_ref[0, 0], a_ref[0, 1]], axis=1)
    w = jnp.einsum("kmn,knc->kmc", g2_ref[...], a2, preferred_element_type=F32)
    wr = w[:, :s2, :].astype(BF16)
    wi = w[:, s2:, :].astype(BF16)
    groups = []
    for g in range(F_GROUPS):
        gs = slice(g * F_GROUP_DIM, (g + 1) * F_GROUP_DIM)
        lhs = jnp.concatenate([wr[:, :, gs], wi[:, :, gs]], axis=-1).reshape(kb * s2, 2 * F_GROUP_DIM)
        groups.append(jnp.dot(lhs, cs_ref[...], preferred_element_type=F32).astype(BF16))
    og = jnp.concatenate(groups, axis=-1)
    out = jnp.dot(perm_ref[...], og, preferred_element_type=F32).astype(BF16)
    o_ref[0] = out.reshape(s2, kb, F_W)


def _fourier_mix(x3, b, s):
    s1, s2 = _dft_factors(s)
    f1, g2, cs, perm = _dft_tables(s)
    nb = min(s2, 8)
    x = x3.reshape(b, s1, s2 * F_W)
    a = pl.pallas_call(
        _dft_seq1_kernel,
        grid=(b, s2 // nb),
        in_specs=[
            pl.BlockSpec((2 * s1, s1), lambda i, j: (0, 0)),
            pl.BlockSpec((1, s1, nb * F_W), lambda i, j: (i, 0, j)),
        ],
        out_specs=pl.BlockSpec((1, 2 * s1, nb * F_W), lambda i, j: (i, 0, j)),
        out_shape=jax.ShapeDtypeStruct((b, 2 * s1, s2 * F_W), BF16),
        compiler_params=pltpu.CompilerParams(
            dimension_semantics=("arbitrary", "arbitrary"),
            vmem_limit_bytes=_vmem_limit(2 * 3 * s1 * nb * F_W * 2 + 2 * s1 * nb * F_W * 4)),
        name="dft_seq1",
    )(f1, x)
    a = a.reshape(b, 2, s1, s2, F_W)
    kb = DFT_BLOCK_ROWS // s2
    out = pl.pallas_call(
        functools.partial(_dft_seq2_kernel, kb=kb, s2=s2),
        grid=(b, s1 // kb),
        in_specs=[
            pl.BlockSpec((kb, 2 * s2, 2 * s2), lambda i, j: (j, 0, 0)),
            pl.BlockSpec((2 * F_GROUP_DIM, F_GROUP_DIM), lambda i, j: (0, 0)),
            pl.BlockSpec((kb * s2, kb * s2), lambda i, j: (0, 0)),
            pl.BlockSpec((1, 2, kb, s2, F_W), lambda i, j: (i, 0, j, 0, 0)),
        ],
        out_specs=pl.BlockSpec((1, s2, kb, F_W), lambda i, j: (i, 0, j, 0)),
        out_shape=jax.ShapeDtypeStruct((b, s2, s1, F_W), BF16),
        compiler_params=pltpu.CompilerParams(
            dimension_semantics=("arbitrary", "arbitrary"),
            vmem_limit_bytes=_vmem_limit(2 * 3 * kb * s2 * F_W * 2 + 4 * kb * s2 * F_W * 4
                                         + 2 * (kb * s2) ** 2 * 2)),
        name="dft_seq2",
    )(g2, cs, perm, a)
    return out.reshape(b * s, F_W)


def _conv_kernel(prev_ref, cur_ref, next_ref, dw_ref, db_ref, lg_ref, lb_ref, o_ref, pad_ref, conv_ref):
    i = pl.program_id(1)
    last = pl.num_programs(1) - 1
    prev = jnp.where(i > 0, prev_ref[0, 0], 0.0)
    nxt = jnp.where(i < last, next_ref[0, 0], 0.0)
    n_lt = C_W // V7X_LANES
    lane_tiles = [slice(c * V7X_LANES, (c + 1) * V7X_LANES) for c in range(n_lt)]
    for c, ls in enumerate(lane_tiles):
        pad_ref[c, 0:CONV_HALO, :] = prev[:, ls]
        pad_ref[c, CONV_HALO:CONV_HALO + CONV_ROWS, :] = cur_ref[0, :, ls]
        pad_ref[c, CONV_HALO + CONV_ROWS:, :] = nxt[:, ls]
    first = CONV_HALO - CONV_PAD
    n_sub = CONV_CHUNK // V7X_SUBLANES
    for c, ls in enumerate(lane_tiles):
        w = [dw_ref[t, :, ls] for t in range(CONV_K)]
        bias = jnp.broadcast_to(db_ref[:, ls], (V7X_SUBLANES, V7X_LANES))

        def chunk_body(ch, carry, c=c, w=w, bias=bias):
            r0 = pl.multiple_of(ch * CONV_CHUNK, CONV_CHUNK)
            acc = [bias for _ in range(n_sub)]
            for t in range(CONV_K):
                for k in range(n_sub):
                    row = r0 + (k * V7X_SUBLANES + first + t)
                    acc[k] = acc[k] + w[t] * pad_ref[c, pl.ds(row, V7X_SUBLANES), :]
            conv_ref[c, pl.ds(r0, CONV_CHUNK), :] = jnp.concatenate(acc, axis=0)
            return carry

        lax.fori_loop(0, CONV_ROWS // CONV_CHUNK, chunk_body, 0)

    for ch in range(CONV_ROWS // CONV_CHUNK):
        r0 = ch * CONV_CHUNK
        accs = [conv_ref[c, r0:r0 + CONV_CHUNK, :] for c in range(n_lt)]
        mu = sum(accs).sum(axis=-1, keepdims=True) * (1.0 / C_W)
        xcs = [a - mu for a in accs]
        var = sum(x * x for x in xcs).sum(axis=-1, keepdims=True) * (1.0 / C_W)
        rstd = lax.rsqrt(var + EPS)
        for c, ls in enumerate(lane_tiles):
            y = (xcs[c] * rstd) * lg_ref[:, ls] + lb_ref[:, ls]
            o_ref[0, r0:r0 + CONV_CHUNK, ls] = (y * jax.nn.sigmoid(y)).astype(BF16)


def _conv_module(glu, b, s, dw, db, lg, lb):
    nblk = s // CONV_ROWS
    per = CONV_ROWS // CONV_HALO
    x3 = glu.reshape(b, s, C_W)
    x4 = glu.reshape(b, s // CONV_HALO, CONV_HALO, C_W)
    dwb = jnp.broadcast_to(dw[:, None, :], (CONV_K, V7X_SUBLANES, C_W))
    vec = lambda i, j: (0, 0)
    out = pl.pallas_call(
        _conv_kernel,
        grid=(b, nblk),
        in_specs=[
            pl.BlockSpec((1, 1, CONV_HALO, C_W), lambda i, j: (i, jnp.maximum(j * per - 1, 0), 0, 0)),
            pl.BlockSpec((1, CONV_ROWS, C_W), lambda i, j: (i, j, 0)),
            pl.BlockSpec((1, 1, CONV_HALO, C_W),
                         lambda i, j: (i, jnp.minimum((j + 1) * per, s // CONV_HALO - 1), 0, 0)),
            pl.BlockSpec((CONV_K, V7X_SUBLANES, C_W), lambda i, j: (0, 0, 0)),
            pl.BlockSpec((1, C_W), vec),
            pl.BlockSpec((1, C_W), vec),
            pl.BlockSpec((1, C_W), vec),
        ],
        out_specs=pl.BlockSpec((1, CONV_ROWS, C_W), lambda i, j: (i, j, 0)),
        out_shape=jax.ShapeDtypeStruct((b, s, C_W), BF16),
        scratch_shapes=[pltpu.VMEM((C_W // V7X_LANES, CONV_ROWS + 2 * CONV_HALO, V7X_LANES), F32),
                        pltpu.VMEM((C_W // V7X_LANES, CONV_ROWS, V7X_LANES), F32)],
        compiler_params=pltpu.CompilerParams(
            dimension_semantics=("arbitrary", "arbitrary"),
            vmem_limit_bytes=_vmem_limit(4 * CONV_ROWS * C_W * 4 + 2 * CONV_K * V7X_SUBLANES * C_W * 4)),
        name="conv",
    )(x4, x3, x4, dwb, db, lg, lb)
    return out.reshape(b * s, C_W)


@functools.lru_cache(maxsize=None)
def _natten_index_tables():
    n_key_rows = K_GROUPS * Q_ROWS
    n_dr, n_dc = 2 * WIN_R - 1, 2 * WIN_C - 1
    i = np.arange(Q_ROWS)[:, None]
    rr = np.arange(n_key_rows)[None, :]
    lo = (np.zeros_like(i), i, np.full_like(i, n_key_rows - WIN_R))
    dr_off = (WIN_R - 1, WIN_R - 1 - Q_ROWS, WIN_R - 1 - 2 * Q_ROWS)
    row_sel = np.zeros((3, Q_ROWS, n_key_rows, n_dr), np.float32)
    for p in range(3):
        row_ok = (rr >= lo[p]) & (rr < lo[p] + WIN_R)
        dr = rr - i + dr_off[p]
        for a in range(Q_ROWS):
            for r in range(n_key_rows):
                if row_ok[a, r]:
                    row_sel[p, a, r, dr[a, r]] = 1.0
    qc = np.arange(GRID_W)[:, None]
    kc = np.arange(GRID_W)[None, :]
    start = np.clip(qc - WIN_C // 2, 0, GRID_W - WIN_C)
    col_ok = (kc >= start) & (kc < start + WIN_C)
    dc = kc - qc + (WIN_C - 1)
    col_sel = np.zeros((n_dc, GRID_W, GRID_W), np.float32)
    for a in range(GRID_W):
        for c in range(GRID_W):
            if col_ok[a, c]:
                col_sel[dc[a, c], a, c] = 1.0
    ok = (row_sel.sum(-1) > 0)[:, :, None, :, None] & col_ok[None, None, :, None, :]
    return row_sel, col_sel, ok.reshape(3, Q_TOK, n_key_rows * GRID_W)


def _natten_bias_table(rpb):
    row_sel, col_sel, ok = _natten_index_tables()
    tab = jnp.einsum("pird,hdc,cqk->phiqrk", row_sel, rpb, col_sel, precision=lax.Precision.HIGHEST)
    tab = tab.reshape(3, N_HEADS, Q_TOK, K_GROUPS * Q_TOK)
    return jnp.where(ok[:, None], tab * LOG2E, NEG_INF)


def _natten_kernel(q_ref, kt0_ref, kt1_ref, kt2_ref, v0_ref, v1_ref, v2_ref, tab_ref, o_ref):
    kt_refs = (kt0_ref, kt1_ref, kt2_ref)
    v_refs = (v0_ref, v1_ref, v2_ref)
    j = pl.program_id(1)
    pat = jnp.where(j == 0, 0, jnp.where(j == pl.num_programs(1) - 1, 2, 1))

    def scores(hd):
        qh = q_ref[:, hd * HEAD_DIM:(hd + 1) * HEAD_DIM]
        return [jnp.dot(qh, kt_refs[g][hd], preferred_element_type=F32)
                + tab_ref[pat, hd, :, g * Q_TOK:(g + 1) * Q_TOK] for g in range(K_GROUPS)]

    def fold_lanes(parts, op):
        acc = parts[0]
        for part in parts[1:]:
            acc = op(acc, part)
        tiles = [acc[:, c * V7X_LANES:(c + 1) * V7X_LANES] for c in range(acc.shape[1] // V7X_LANES)]
        out = tiles[0]
        for tile in tiles[1:]:
            out = op(out, tile)
        return out

    sc_next = scores(0)
    for hd in range(N_HEADS):
        hs = slice(hd * HEAD_DIM, (hd + 1) * HEAD_DIM)
        sc = sc_next
        if hd + 1 < N_HEADS:
            sc_next = scores(hd + 1)
        m = fold_lanes(sc, jnp.maximum).max(axis=-1, keepdims=True)
        p = [jnp.exp2(s - m) for s in sc]
        denom = fold_lanes(p, jnp.add).sum(axis=-1, keepdims=True)
        o = jnp.dot(p[0].astype(BF16), v_refs[0][:, hs], preferred_element_type=F32)
        for g in range(1, K_GROUPS):
            o += jnp.dot(p[g].astype(BF16), v_refs[g][:, hs], preferred_element_type=F32)
        o_ref[:, hs] = (o * (1.0 / denom)).astype(BF16)


def _natten(q, kt, v, tab, b, s):
    ng = s // Q_TOK
    assert ng >= K_GROUPS

    def q_map(i, j):
        return (i * ng + j, 0)

    def window_block(i, j, off):
        return i * ng + jnp.clip(j - 1, 0, ng - K_GROUPS) + off

    blk = (Q_TOK, A_W)
    tab_shape = (3, N_HEADS, Q_TOK, K_GROUPS * Q_TOK)
    vmem = 2 * (2 + 2 * K_GROUPS) * Q_TOK * A_W * 2 + math.prod(tab_shape) * 4
    return pl.pallas_call(
        _natten_kernel,
        grid=(b, ng),
        in_specs=[pl.BlockSpec(blk, q_map)]
        + [pl.BlockSpec((None, N_HEADS, HEAD_DIM, Q_TOK), lambda i, j, g=g: (window_block(i, j, g), 0, 0, 0))
           for g in range(K_GROUPS)]
        + [pl.BlockSpec(blk, lambda i, j, g=g: (window_block(i, j, g), 0)) for g in range(K_GROUPS)]
        + [pl.BlockSpec(tab_shape, lambda i, j: (0, 0, 0, 0), pipeline_mode=pl.Buffered(1))],
        out_specs=pl.BlockSpec(blk, q_map),
        out_shape=jax.ShapeDtypeStruct((b * s, A_W), BF16),
        compiler_params=pltpu.CompilerParams(
            dimension_semantics=("arbitrary", "arbitrary"), vmem_limit_bytes=_vmem_limit(vmem)),
        name="natten",
    )(q, kt, kt, kt, v, v, v, tab)


def _merge_kernel(x_ref, g1_ref, fm_ref, uc_ref, at_ref, wg0_ref, wg1_ref, wg2_ref,
                  bg0_ref, bg1_ref, bg2_ref, wf_ref, wc_ref, wa_ref, wo_ref, o_ref, h_ref):
    @pl.when(pl.program_id(1) == 0)
    def _():
        x = x_ref[...]
        h_ref[...] = _rms_norm_bf16(x, g1_ref[...])
        o_ref[...] = x

    h = h_ref[...]

    def gate(w_ref, b_ref):
        return jax.nn.sigmoid(jnp.dot(h, w_ref[...], preferred_element_type=F32) + b_ref[...])

    merged = gate(wg0_ref, bg0_ref) * jnp.dot(fm_ref[...], wf_ref[...], preferred_element_type=F32)
    merged += gate(wg1_ref, bg1_ref) * jnp.dot(uc_ref[...], wc_ref[...], preferred_element_type=F32)
    merged += gate(wg2_ref, bg2_ref) * jnp.dot(at_ref[...], wa_ref[...], preferred_element_type=F32)
    o_ref[...] += jnp.dot(merged.astype(BF16), wo_ref[...], preferred_element_type=F32)


def _merge(x, layer, g1, fm, uc, at, w_in, b_gate, w_f, w_c, w_a, w_o):
    t = x.shape[0]
    tm, tn = TOKEN_TILE, MERGE_COLS
    nj = D_MODEL // tn
    gate_blk = OFF_G // tn
    row = lambda i, j: (i, 0)
    col = lambda i, j: (layer, 0, j)
    gate_specs = [pl.BlockSpec((None, D_MODEL, tn), lambda i, j, br=br: (layer, 0, gate_blk + br * nj + j))
                  for br in range(N_BRANCH)]
    bias_specs = [pl.BlockSpec((1, tn), lambda i, j, br=br: (0, br * nj + j)) for br in range(N_BRANCH)]
    vmem = (4 * tm * D_MODEL * 4 + 2 * tm * (F_W + C_W + A_W) * 2 + tm * D_MODEL * 2
            + 2 * (3 * D_MODEL + F_W + C_W + A_W + D_MODEL) * tn * 2 + 6 * tm * tn * 4)
    return pl.pallas_call(
        _merge_kernel,
        grid=(t // tm, nj),
        in_specs=[
            pl.BlockSpec((tm, D_MODEL), row),
            pl.BlockSpec((1, D_MODEL), lambda i, j: (0, 0)),
            pl.BlockSpec((tm, F_W), row),
            pl.BlockSpec((tm, C_W), row),
            pl.BlockSpec((tm, A_W), row),
            *gate_specs,
            *bias_specs,
            pl.BlockSpec((None, F_W, tn), col),
            pl.BlockSpec((None, C_W, tn), col),
            pl.BlockSpec((None, A_W, tn), col),
            pl.BlockSpec((None, tn, D_MODEL), lambda i, j: (layer, j, 0)),
        ],
        out_specs=pl.BlockSpec((tm, D_MODEL), row),
        out_shape=jax.ShapeDtypeStruct((t, D_MODEL), F32),
        scratch_shapes=[pltpu.VMEM((tm, D_MODEL), BF16)],
        compiler_params=pltpu.CompilerParams(
            dimension_semantics=("arbitrary", "arbitrary"), vmem_limit_bytes=_vmem_limit(vmem)),
        name="merge",
    )(x, g1, fm, uc, at, w_in, w_in, w_in, b_gate, b_gate, b_gate, w_f, w_c, w_a, w_o)


def _mlp_kernel(x_ref, g2_ref, w1_ref, w2_ref, o_ref, h_ref):
    @pl.when(pl.program_id(1) == 0)
    def _():
        x = x_ref[...]
        h_ref[...] = _rms_norm_bf16(x, g2_ref[...])
        o_ref[...] = x

    a = jnp.maximum(jnp.dot(h_ref[...], w1_ref[...], preferred_element_type=F32), 0.0)
    o_ref[...] += jnp.dot((a * a).astype(BF16), w2_ref[...], preferred_element_type=F32)


def _mlp(x, layer, g2, w1, w2):
    t = x.shape[0]
    tm, tf = MLP_TOKEN_TILE, MLP_COLS
    row = lambda i, j: (i, 0)
    vmem = 4 * tm * D_MODEL * 4 + tm * D_MODEL * 2 + 2 * 2 * D_MODEL * tf * 2 + 2 * tm * tf * 4
    return pl.pallas_call(
        _mlp_kernel,
        grid=(t // tm, D_FF // tf),
        in_specs=[
            pl.BlockSpec((tm, D_MODEL), row),
            pl.BlockSpec((1, D_MODEL), lambda i, j: (0, 0)),
            pl.BlockSpec((None, D_MODEL, tf), lambda i, j: (layer, 0, j)),
            pl.BlockSpec((None, tf, D_MODEL), lambda i, j: (layer, j, 0)),
        ],
        out_specs=pl.BlockSpec((tm, D_MODEL), row),
        out_shape=jax.ShapeDtypeStruct((t, D_MODEL), F32),
        scratch_shapes=[pltpu.VMEM((tm, D_MODEL), BF16)],
        compiler_params=pltpu.CompilerParams(
            dimension_semantics=("arbitrary", "arbitrary"), vmem_limit_bytes=_vmem_limit(vmem)),
        name="mlp",
    )(x, g2, w1, w2)


def _layer(x, b, s, l, w, p):
    _, s2 = _dft_factors(s)
    x3, glu, q, kt, v = _in_proj(x, l, p["n1"], w["w_in"], p["qg"], p["kg"], s2)
    fm = _fourier_mix(x3, b, s)
    uc = _conv_module(glu, b, s, p["dw"], p["db"], p["ln_g"], p["ln_b"])
    at = _natten(q, kt, v, p["tab"], b, s)
    x = _merge(x, l, p["n1"], fm, uc, at, w["w_in"], p["b_g"], w["w_f"], w["w_c"], w["w_a"], w["w_o"])
    return _mlp(x, l, p["n2"], w["w1"], w["w2"])


def kernel(x_prompt, x_sample, norm1_g, w_in, b_gate, w_fourier, conv_dw, conv_db, conv_ln_g, conv_ln_b,
           w_conv_out, q_norm_g, k_norm_g, rpb, w_attn_out, w_out, norm2_g, w_mlp_in, w_mlp_out):
    depth = w_in.shape[0]
    w = dict(w_in=w_in.astype(BF16), w_f=w_fourier.astype(BF16), w_c=w_conv_out.astype(BF16),
             w_a=w_attn_out.astype(BF16), w_o=w_out.astype(BF16),
             w1=w_mlp_in.astype(BF16), w2=w_mlp_out.astype(BF16))
    layers = []
    for l in range(depth):
        layers.append(dict(
            n1=norm1_g[l][None, :], n2=norm2_g[l][None, :], b_g=b_gate[l][None, :],
            dw=conv_dw[l], db=conv_db[l][None, :], ln_g=conv_ln_g[l][None, :], ln_b=conv_ln_b[l][None, :],
            qg=q_norm_g[l][None, :], kg=k_norm_g[l][None, :],
            tab=_natten_bias_table(rpb[l]),
        ))

    def trunk(x):
        b, s, d = x.shape
        y = x.reshape(b * s, d)
        for l, p in enumerate(layers):
            y = _layer(y, b, s, l, w, p)
        return y.reshape(b, s, d)

    return (trunk(x_prompt), trunk(x_sample))
```

```python
import functools
import math

import numpy as np
import jax
import jax.numpy as jnp
from jax import lax
from jax.experimental import pallas as pl
from jax.experimental.pallas import tpu as pltpu

D_MODEL = 2048
GRID_W = 64
F_GROUPS = 4
F_GROUP_DIM = 128
F_W = F_GROUPS * F_GROUP_DIM
C_W = 512
CONV_K = 31
CONV_PAD = CONV_K // 2
N_HEADS = 8
HEAD_DIM = 128
A_W = N_HEADS * HEAD_DIM
WIN_R = 8
WIN_C = 16
N_BRANCH = 3
OFF_F = 0
OFF_CA = OFF_F + F_W
OFF_CG = OFF_CA + C_W
OFF_Q = OFF_CG + C_W
OFF_K = OFF_Q + A_W
OFF_V = OFF_K + A_W
OFF_G = OFF_V + A_W
IN_COLS = OFF_G + N_BRANCH * D_MODEL
D_FF = 4 * D_MODEL
EPS = 1e-6
NEG_INF = -1e30
LOG2E = math.log2(math.e)

BF16 = jnp.bfloat16
F32 = jnp.float32

V7X_VMEM_BYTES = 64 * 1024 * 1024
V7X_LANES = 128
V7X_SUBLANES = 8

TOKEN_TILE = 512
MERGE_COLS = 512
MLP_COLS = 1024
CONV_ROWS = 256
CONV_HALO = 16
CONV_CHUNK = 32
DFT_BLOCK_ROWS = 512
Q_ROWS = 4
K_GROUPS = 3
Q_TOK = Q_ROWS * GRID_W


def _vmem_limit(nbytes):
    return int(min(nbytes * 5 // 4 + (8 << 20), V7X_VMEM_BYTES - (4 << 20)))


def _rms_norm_bf16(x, gain):
    ms = jnp.mean(x * x, axis=-1, keepdims=True)
    return ((x * lax.rsqrt(ms + EPS)) * gain).astype(BF16)


def _in_proj_kernel(x_ref, g1_ref, w_ref, qg_ref, kg_ref, h_ref, f_ref, glu_ref, q_ref, kt_ref, v_ref, zf_ref,
                    *, s2):
    h = _rms_norm_bf16(x_ref[...], g1_ref[...])
    h_ref[...] = h

    def proj(c0, n):
        return jnp.dot(h, w_ref[:, c0:c0 + n], preferred_element_type=F32)

    zf = proj(OFF_F, F_W)
    n_lt = F_W // V7X_LANES
    for c in range(n_lt):
        zf_ref[c] = zf[:, c * V7X_LANES:(c + 1) * V7X_LANES]
    rows = zf.shape[0] // s2
    for n2 in range(s2):
        for c in range(n_lt):
            piece = zf_ref[c, pl.ds(n2, rows, stride=s2), :]
            lane0 = n2 * F_W + c * V7X_LANES
            f_ref[:, lane0:lane0 + V7X_LANES] = piece.astype(BF16)

    a = proj(OFF_CA, C_W)
    g = proj(OFF_CG, C_W)
    glu_ref[...] = a * jax.nn.sigmoid(g)

    def head_norm(z, gain, scale, store):
        for hd in range(N_HEADS):
            zh = z[:, hd * HEAD_DIM:(hd + 1) * HEAD_DIM]
            ms = jnp.mean(zh * zh, axis=-1, keepdims=True)
            y = (zh * lax.rsqrt(ms + EPS)) * gain
            if scale != 1.0:
                y = y * scale
            store(hd, y)

    def store_q(hd, y):
        q_ref[:, hd * HEAD_DIM:(hd + 1) * HEAD_DIM] = y.astype(BF16)

    def store_k_transposed(hd, y):
        yt = jnp.transpose(y).astype(BF16)
        for blk in range(y.shape[0] // Q_TOK):
            kt_ref[blk, hd] = yt[:, blk * Q_TOK:(blk + 1) * Q_TOK]

    head_norm(proj(OFF_Q, A_W), qg_ref[...], LOG2E / math.sqrt(HEAD_DIM), store_q)
    head_norm(proj(OFF_K, A_W), kg_ref[...], 1.0, store_k_transposed)
    v_ref[...] = proj(OFF_V, A_W).astype(BF16)


def _in_proj(x, layer, g1, w_in, qg, kg, s2):
    t = x.shape[0]
    tm = TOKEN_TILE
    row = lambda i: (i, 0)
    fixed = lambda i: (0, 0)
    vmem = (2 * tm * D_MODEL * 4 + D_MODEL * OFF_G * 2
            + 2 * tm * (D_MODEL * 2 + F_W * 2 + C_W * 4 + 3 * A_W * 2) + 2 * tm * A_W * 4 + tm * F_W * 4)
    return pl.pallas_call(
        functools.partial(_in_proj_kernel, s2=s2),
        grid=(t // tm,),
        in_specs=[
            pl.BlockSpec((tm, D_MODEL), row),
            pl.BlockSpec((1, D_MODEL), fixed),
            pl.BlockSpec((None, D_MODEL, OFF_G), lambda i: (layer, 0, 0), pipeline_mode=pl.Buffered(1)),
            pl.BlockSpec((1, HEAD_DIM), fixed),
            pl.BlockSpec((1, HEAD_DIM), fixed),
        ],
        out_specs=[
            pl.BlockSpec((tm, D_MODEL), row),
            pl.BlockSpec((tm // s2, s2 * F_W), row),
            pl.BlockSpec((tm, C_W), row),
            pl.BlockSpec((tm, A_W), row),
            pl.BlockSpec((tm // Q_TOK, N_HEADS, HEAD_DIM, Q_TOK), lambda i: (i, 0, 0, 0)),
            pl.BlockSpec((tm, A_W), row),
        ],
        out_shape=[
            jax.ShapeDtypeStruct((t, D_MODEL), BF16),
            jax.ShapeDtypeStruct((t // s2, s2 * F_W), BF16),
            jax.ShapeDtypeStruct((t, C_W), F32),
            jax.ShapeDtypeStruct((t, A_W), BF16),
            jax.ShapeDtypeStruct((t // Q_TOK, N_HEADS, HEAD_DIM, Q_TOK), BF16),
            jax.ShapeDtypeStruct((t, A_W), BF16),
        ],
        scratch_shapes=[pltpu.VMEM((F_W // V7X_LANES, tm, V7X_LANES), F32)],
        compiler_params=pltpu.CompilerParams(
            dimension_semantics=("arbitrary",), vmem_limit_bytes=_vmem_limit(vmem)),
        name="in_proj",
    )(x, g1, w_in, qg, kg)


def _dft_factors(s):
    s1 = 256 if s >= 8192 else 128
    return s1, s // s1


@functools.lru_cache(maxsize=None)
def _dft_tables(s):
    s1, s2 = _dft_factors(s)
    n1 = np.arange(s1)
    ang1 = 2.0 * np.pi * np.outer(n1, n1) / s1
    f1 = np.concatenate([np.cos(ang1), -np.sin(ang1)], axis=0) / np.sqrt(s1)
    k1 = np.arange(s1)[:, None, None]
    k2 = np.arange(s2)[None, :, None]
    n2 = np.arange(s2)[None, None, :]
    ang2 = 2.0 * np.pi * ((n2 * (k1 + s1 * k2)) % s) / s
    gr = np.cos(ang2) / np.sqrt(s2)
    gi = -np.sin(ang2) / np.sqrt(s2)
    g2 = np.concatenate([np.concatenate([gr, -gi], axis=2),
                         np.concatenate([gi, gr], axis=2)], axis=1)
    c = np.arange(F_GROUP_DIM)
    ang3 = 2.0 * np.pi * np.outer(c, c) / F_GROUP_DIM
    cs = np.concatenate([np.cos(ang3), np.sin(ang3)], axis=0) / np.sqrt(F_GROUP_DIM)
    kb = DFT_BLOCK_ROWS // s2
    perm = np.zeros((kb * s2, kb * s2), np.float32)
    for a in range(kb):
        for r in range(s2):
            perm[r * kb + a, a * s2 + r] = 1.0
    as_bf16 = lambda a: jnp.asarray(a.astype(np.float32)).astype(BF16)
    return as_bf16(f1), as_bf16(g2), as_bf16(cs), as_bf16(perm)


def _dft_seq1_kernel(f1_ref, x_ref, a_ref):
    a_ref[0] = jnp.dot(f1_ref[...], x_ref[0], preferred_element_type=F32).astype(BF16)


def _dft_seq2_kernel(g2_ref, cs_ref, perm_ref, a_ref, o_ref, *, kb, s2):
    a2 = jnp.concatenate([a_ref[0, 0], a_ref[0, 1]], axis=1)
    w = jnp.einsum("kmn,knc->kmc", g2_ref[...], a2, preferred_element_type=F32)
    wr = w[:, :s2, :].astype(BF16)
    wi = w[:, s2:, :].astype(BF16)
    groups = []
    for g in range(F_GROUPS):
        gs = slice(g * F_GROUP_DIM, (g + 1) * F_GROUP_DIM)
        lhs = jnp.concatenate([wr[:, :, gs], wi[:, :, gs]], axis=-1).reshape(kb * s2, 2 * F_GROUP_DIM)
        groups.append(jnp.dot(lhs, cs_ref[...], preferred_element_type=F32).astype(BF16))
    og = jnp.concatenate(groups, axis=-1)
    out = jnp.dot(perm_ref[...], og, preferred_element_type=F32).astype(BF16)
    o_ref[0] = out.reshape(s2, kb, F_W)


def _fourier_mix(x3, b, s):
    s1, s2 = _dft_factors(s)
    f1, g2, cs, perm = _dft_tables(s)
    nb = min(s2, 8)
    x = x3.reshape(b, s1, s2 * F_W)
    a = pl.pallas_call(
        _dft_seq1_kernel,
        grid=(b, s2 // nb),
        in_specs=[
            pl.BlockSpec((2 * s1, s1), lambda i, j: (0, 0)),
            pl.BlockSpec((1, s1, nb * F_W), lambda i, j: (i, 0, j)),
        ],
        out_specs=pl.BlockSpec((1, 2 * s1, nb * F_W), lambda i, j: (i, 0, j)),
        out_shape=jax.ShapeDtypeStruct((b, 2 * s1, s2 * F_W), BF16),
        compiler_params=pltpu.CompilerParams(
            dimension_semantics=("arbitrary", "arbitrary"),
            vmem_limit_bytes=_vmem_limit(2 * 3 * s1 * nb * F_W * 2 + 2 * s1 * nb * F_W * 4)),
        name="dft_seq1",
    )(f1, x)
    a = a.reshape(b, 2, s1, s2, F_W)
    kb = DFT_BLOCK_ROWS // s2
    out = pl.pallas_call(
        functools.partial(_dft_seq2_kernel, kb=kb, s2=s2),
        grid=(b, s1 // kb),
        in_specs=[
            pl.BlockSpec((kb, 2 * s2, 2 * s2), lambda i, j: (j, 0, 0)),
            pl.BlockSpec((2 * F_GROUP_DIM, F_GROUP_DIM), lambda i, j: (0, 0)),
            pl.BlockSpec((kb * s2, kb * s2), lambda i, j: (0, 0)),
            pl.BlockSpec((1, 2, kb, s2, F_W), lambda i, j: (i, 0, j, 0, 0)),
        ],
        out_specs=pl.BlockSpec((1, s2, kb, F_W), lambda i, j: (i, 0, j, 0)),
        out_shape=jax.ShapeDtypeStruct((b, s2, s1, F_W), BF16),
        compiler_params=pltpu.CompilerParams(
            dimension_semantics=("arbitrary", "arbitrary"),
            vmem_limit_bytes=_vmem_limit(2 * 3 * kb * s2 * F_W * 2 + 4 * kb * s2 * F_W * 4
                                         + 2 * (kb * s2) ** 2 * 2)),
        name="dft_seq2",
    )(g2, cs, perm, a)
    return out.reshape(b * s, F_W)


def _conv_kernel(prev_ref, cur_ref, next_ref, dw_ref, db_ref, lg_ref, lb_ref, o_ref, pad_ref, conv_ref):
    i = pl.program_id(1)
    last = pl.num_programs(1) - 1
    prev = jnp.where(i > 0, prev_ref[0, 0], 0.0)
    nxt = jnp.where(i < last, next_ref[0, 0], 0.0)
    n_lt = C_W // V7X_LANES
    lane_tiles = [slice(c * V7X_LANES, (c + 1) * V7X_LANES) for c in range(n_lt)]
    for c, ls in enumerate(lane_tiles):
        pad_ref[c, 0:CONV_HALO, :] = prev[:, ls]
        pad_ref[c, CONV_HALO:CONV_HALO + CONV_ROWS, :] = cur_ref[0, :, ls]
        pad_ref[c, CONV_HALO + CONV_ROWS:, :] = nxt[:, ls]
    first = CONV_HALO - CONV_PAD
    n_sub = CONV_CHUNK // V7X_SUBLANES
    for c, ls in enumerate(lane_tiles):
        w = [dw_ref[t, :, ls] for t in range(CONV_K)]
        bias = jnp.broadcast_to(db_ref[:, ls], (V7X_SUBLANES, V7X_LANES))

        def chunk_body(ch, carry, c=c, w=w, bias=bias):
            r0 = pl.multiple_of(ch * CONV_CHUNK, CONV_CHUNK)
            acc = [bias for _ in range(n_sub)]
            for t in range(CONV_K):
                for k in range(n_sub):
                    row = r0 + (k * V7X_SUBLANES + first + t)
                    acc[k] = acc[k] + w[t] * pad_ref[c, pl.ds(row, V7X_SUBLANES), :]
            conv_ref[c, pl.ds(r0, CONV_CHUNK), :] = jnp.concatenate(acc, axis=0)
            return carry

        lax.fori_loop(0, CONV_ROWS // CONV_CHUNK, chunk_body, 0)

    for ch in range(CONV_ROWS // CONV_CHUNK):
        r0 = ch * CONV_CHUNK
        accs = [conv_ref[c, r0:r0 + CONV_CHUNK, :] for c in range(n_lt)]
        mu = sum(accs).sum(axis=-1, keepdims=True) * (1.0 / C_W)
        xcs = [a - mu for a in accs]
        var = sum(x * x for x in xcs).sum(axis=-1, keepdims=True) * (1.0 / C_W)
        rstd = lax.rsqrt(var + EPS)
        for c, ls in enumerate(lane_tiles):
            y = (xcs[c] * rstd) * lg_ref[:, ls] + lb_ref[:, ls]
            o_ref[0, r0:r0 + CONV_CHUNK, ls] = (y * jax.nn.sigmoid(y)).astype(BF16)


def _conv_module(glu, b, s, dw, db, lg, lb):
    nblk = s // CONV_ROWS
    per = CONV_ROWS // CONV_HALO
    x3 = glu.reshape(b, s, C_W)
    x4 = glu.reshape(b, s // CONV_HALO, CONV_HALO, C_W)
    dwb = jnp.broadcast_to(dw[:, None, :], (CONV_K, V7X_SUBLANES, C_W))
    vec = lambda i, j: (0, 0)
    out = pl.pallas_call(
        _conv_kernel,
        grid=(b, nblk),
        in_specs=[
            pl.BlockSpec((1, 1, CONV_HALO, C_W), lambda i, j: (i, jnp.maximum(j * per - 1, 0), 0, 0)),
            pl.BlockSpec((1, CONV_ROWS, C_W), lambda i, j: (i, j, 0)),
            pl.BlockSpec((1, 1, CONV_HALO, C_W),
                         lambda i, j: (i, jnp.minimum((j + 1) * per, s // CONV_HALO - 1), 0, 0)),
            pl.BlockSpec((CONV_K, V7X_SUBLANES, C_W), lambda i, j: (0, 0, 0)),
            pl.BlockSpec((1, C_W), vec),
            pl.BlockSpec((1, C_W), vec),
            pl.BlockSpec((1, C_W), vec),
        ],
        out_specs=pl.BlockSpec((1, CONV_ROWS, C_W), lambda i, j: (i, j, 0)),
        out_shape=jax.ShapeDtypeStruct((b, s, C_W), BF16),
        scratch_shapes=[pltpu.VMEM((C_W // V7X_LANES, CONV_ROWS + 2 * CONV_HALO, V7X_LANES), F32),
                        pltpu.VMEM((C_W // V7X_LANES, CONV_ROWS, V7X_LANES), F32)],
        compiler_params=pltpu.CompilerParams(
            dimension_semantics=("arbitrary", "arbitrary"),
            vmem_limit_bytes=_vmem_limit(4 * CONV_ROWS * C_W * 4 + 2 * CONV_K * V7X_SUBLANES * C_W * 4)),
        name="conv",
    )(x4, x3, x4, dwb, db, lg, lb)
    return out.reshape(b * s, C_W)


@functools.lru_cache(maxsize=None)
def _natten_index_tables():
    n_key_rows = K_GROUPS * Q_ROWS
    n_dr, n_dc = 2 * WIN_R - 1, 2 * WIN_C - 1
    i = np.arange(Q_ROWS)[:, None]
    rr = np.arange(n_key_rows)[None, :]
    lo = (np.zeros_like(i), i, np.full_like(i, n_key_rows - WIN_R))
    dr_off = (WIN_R - 1, WIN_R - 1 - Q_ROWS, WIN_R - 1 - 2 * Q_ROWS)
    row_sel = np.zeros((3, Q_ROWS, n_key_rows, n_dr), np.float32)
    for p in range(3):
        row_ok = (rr >= lo[p]) & (rr < lo[p] + WIN_R)
        dr = rr - i + dr_off[p]
        for a in range(Q_ROWS):
            for r in range(n_key_rows):
                if row_ok[a, r]:
                    row_sel[p, a, r, dr[a, r]] = 1.0
    qc = np.arange(GRID_W)[:, None]
    kc = np.arange(GRID_W)[None, :]
    start = np.clip(qc - WIN_C // 2, 0, GRID_W - WIN_C)
    col_ok = (kc >= start) & (kc < start + WIN_C)
    dc = kc - qc + (WIN_C - 1)
    col_sel = np.zeros((n_dc, GRID_W, GRID_W), np.float32)
    for a in range(GRID_W):
        for c in range(GRID_W):
            if col_ok[a, c]:
                col_sel[dc[a, c], a, c] = 1.0
    ok = (row_sel.sum(-1) > 0)[:, :, None, :, None] & col_ok[None, None, :, None, :]
    return row_sel, col_sel, ok.reshape(3, Q_TOK, n_key_rows * GRID_W)


def _natten_bias_table(rpb):
    row_sel, col_sel, ok = _natten_index_tables()
    tab = jnp.einsum("pird,hdc,cqk->phiqrk", row_sel, rpb, col_sel, precision=lax.Precision.HIGHEST)
    tab = tab.reshape(3, N_HEADS, Q_TOK, K_GROUPS * Q_TOK)
    return jnp.where(ok[:, None], tab * LOG2E, NEG_INF)


def _natten_kernel(q_ref, kt0_ref, kt1_ref, kt2_ref, v0_ref, v1_ref, v2_ref, tab_ref, o_ref):
    kt_refs = (kt0_ref, kt1_ref, kt2_ref)
    v_refs = (v0_ref, v1_ref, v2_ref)
    j = pl.program_id(1)
    pat = jnp.where(j == 0, 0, jnp.where(j == pl.num_programs(1) - 1, 2, 1))

    def scores(hd):
        qh = q_ref[:, hd * HEAD_DIM:(hd + 1) * HEAD_DIM]
        return [jnp.dot(qh, kt_refs[g][hd], preferred_element_type=F32)
                + tab_ref[pat, hd, :, g * Q_TOK:(g + 1) * Q_TOK] for g in range(K_GROUPS)]

    def fold_lanes(parts, op):
        acc = parts[0]
        for part in parts[1:]:
            acc = op(acc, part)
        tiles = [acc[:, c * V7X_LANES:(c + 1) * V7X_LANES] for c in range(acc.shape[1] // V7X_LANES)]
        out = tiles[0]
        for tile in tiles[1:]:
            out = op(out, tile)
        return out

    sc_next = scores(0)
    for hd in range(N_HEADS):
        hs = slice(hd * HEAD_DIM, (hd + 1) * HEAD_DIM)
        sc = sc_next
        if hd + 1 < N_HEADS:
            sc_next = scores(hd + 1)
        m = fold_lanes(sc, jnp.maximum).max(axis=-1, keepdims=True)
        p = [jnp.exp2(s - m) for s in sc]
        denom = fold_lanes(p, jnp.add).sum(axis=-1, keepdims=True)
        o = jnp.dot(p[0].astype(BF16), v_refs[0][:, hs], preferred_element_type=F32)
        for g in range(1, K_GROUPS):
            o += jnp.dot(p[g].astype(BF16), v_refs[g][:, hs], preferred_element_type=F32)
        o_ref[:, hs] = (o * (1.0 / denom)).astype(BF16)


def _natten(q, kt, v, tab, b, s):
    ng = s // Q_TOK
    assert ng >= K_GROUPS

    def q_map(i, j):
        return (i * ng + j, 0)

    def window_block(i, j, off):
        return i * ng + jnp.clip(j - 1, 0, ng - K_GROUPS) + off

    blk = (Q_TOK, A_W)
    tab_shape = (3, N_HEADS, Q_TOK, K_GROUPS * Q_TOK)
    vmem = 2 * (2 + 2 * K_GROUPS) * Q_TOK * A_W * 2 + math.prod(tab_shape) * 4
    return pl.pallas_call(
        _natten_kernel,
        grid=(b, ng),
        in_specs=[pl.BlockSpec(blk, q_map)]
        + [pl.BlockSpec((None, N_HEADS, HEAD_DIM, Q_TOK), lambda i, j, g=g: (window_block(i, j, g), 0, 0, 0))
           for g in range(K_GROUPS)]
        + [pl.BlockSpec(blk, lambda i, j, g=g: (window_block(i, j, g), 0)) for g in range(K_GROUPS)]
        + [pl.BlockSpec(tab_shape, lambda i, j: (0, 0, 0, 0), pipeline_mode=pl.Buffered(1))],
        out_specs=pl.BlockSpec(blk, q_map),
        out_shape=jax.ShapeDtypeStruct((b * s, A_W), BF16),
        compiler_params=pltpu.CompilerParams(
            dimension_semantics=("arbitrary", "arbitrary"), vmem_limit_bytes=_vmem_limit(vmem)),
        name="natten",
    )(q, kt, kt, kt, v, v, v, tab)


def _merge_kernel(x_ref, h_ref, fm_ref, uc_ref, at_ref, wg0_ref, wg1_ref, wg2_ref,
                  bg0_ref, bg1_ref, bg2_ref, wf_ref, wc_ref, wa_ref, wo_ref, o_ref):
    def gate(w_ref, b_ref):
        return jax.nn.sigmoid(jnp.dot(h_ref[...], w_ref[...], preferred_element_type=F32) + b_ref[...])

    def step(acc_ref):
        merged = gate(wg0_ref, bg0_ref) * jnp.dot(fm_ref[...], wf_ref[...], preferred_element_type=F32)
        merged += gate(wg1_ref, bg1_ref) * jnp.dot(uc_ref[...], wc_ref[...], preferred_element_type=F32)
        merged += gate(wg2_ref, bg2_ref) * jnp.dot(at_ref[...], wa_ref[...], preferred_element_type=F32)
        o_ref[...] = acc_ref[...] + jnp.dot(merged.astype(BF16), wo_ref[...], preferred_element_type=F32)

    first = pl.program_id(1) == 0
    pl.when(first)(functools.partial(step, x_ref))
    pl.when(jnp.logical_not(first))(functools.partial(step, o_ref))


def _merge(x, layer, h, fm, uc, at, w_in, b_gate, w_f, w_c, w_a, w_o):
    t = x.shape[0]
    tm, tn = TOKEN_TILE, MERGE_COLS
    nj = D_MODEL // tn
    gate_blk = OFF_G // tn
    row = lambda i, j: (i, 0)
    col = lambda i, j: (layer, 0, j)
    gate_specs = [pl.BlockSpec((None, D_MODEL, tn), lambda i, j, br=br: (layer, 0, gate_blk + br * nj + j))
                  for br in range(N_BRANCH)]
    bias_specs = [pl.BlockSpec((1, tn), lambda i, j, br=br: (0, br * nj + j)) for br in range(N_BRANCH)]
    vmem = (4 * tm * D_MODEL * 4 + 2 * tm * (D_MODEL + F_W + C_W + A_W) * 2
            + 2 * (3 * D_MODEL + F_W + C_W + A_W + D_MODEL) * tn * 2 + 6 * tm * tn * 4)
    return pl.pallas_call(
        _merge_kernel,
        grid=(t // tm, nj),
        in_specs=[
            pl.BlockSpec((tm, D_MODEL), row),
            pl.BlockSpec((tm, D_MODEL), row),
            pl.BlockSpec((tm, F_W), row),
            pl.BlockSpec((tm, C_W), row),
            pl.BlockSpec((tm, A_W), row),
            *gate_specs,
            *bias_specs,
            pl.BlockSpec((None, F_W, tn), col),
            pl.BlockSpec((None, C_W, tn), col),
            pl.BlockSpec((None, A_W, tn), col),
            pl.BlockSpec((None, tn, D_MODEL), lambda i, j: (layer, j, 0)),
        ],
        out_specs=pl.BlockSpec((tm, D_MODEL), row),
        out_shape=jax.ShapeDtypeStruct((t, D_MODEL), F32),
        compiler_params=pltpu.CompilerParams(
            dimension_semantics=("arbitrary", "arbitrary"), vmem_limit_bytes=_vmem_limit(vmem)),
        name="merge",
    )(x, h, fm, uc, at, w_in, w_in, w_in, b_gate, b_gate, b_gate, w_f, w_c, w_a, w_o)


def _mlp_kernel(x_ref, g2_ref, w1_ref, w2_ref, o_ref, h_ref):
    def step(acc_ref):
        a = jnp.maximum(jnp.dot(h_ref[...], w1_ref[...], preferred_element_type=F32), 0.0)
        o_ref[...] = acc_ref[...] + jnp.dot((a * a).astype(BF16), w2_ref[...], preferred_element_type=F32)

    @pl.when(pl.program_id(1) == 0)
    def _():
        h_ref[...] = _rms_norm_bf16(x_ref[...], g2_ref[...])
        step(x_ref)

    pl.when(pl.program_id(1) > 0)(functools.partial(step, o_ref))


def _mlp(x, layer, g2, w1, w2):
    t = x.shape[0]
    tm, tf = TOKEN_TILE, MLP_COLS
    row = lambda i, j: (i, 0)
    vmem = 4 * tm * D_MODEL * 4 + tm * D_MODEL * 2 + 2 * 2 * D_MODEL * tf * 2 + 2 * tm * tf * 4
    return pl.pallas_call(
        _mlp_kernel,
        grid=(t // tm, D_FF // tf),
        in_specs=[
            pl.BlockSpec((tm, D_MODEL), row),
            pl.BlockSpec((1, D_MODEL), lambda i, j: (0, 0)),
            pl.BlockSpec((None, D_MODEL, tf), lambda i, j: (layer, 0, j)),
            pl.BlockSpec((None, tf, D_MODEL), lambda i, j: (layer, j, 0)),
        ],
        out_specs=pl.BlockSpec((tm, D_MODEL), row),
        out_shape=jax.ShapeDtypeStruct((t, D_MODEL), F32),
        scratch_shapes=[pltpu.VMEM((tm, D_MODEL), BF16)],
        compiler_params=pltpu.CompilerParams(
            dimension_semantics=("arbitrary", "arbitrary"), vmem_limit_bytes=_vmem_limit(vmem)),
        name="mlp",
    )(x, g2, w1, w2)


def _layer(x, b, s, l, w, p):
    _, s2 = _dft_factors(s)
    h, x3, glu, q, kt, v = _in_proj(x, l, p["n1"], w["w_in"], p["qg"], p["kg"], s2)
    fm = _fourier_mix(x3, b, s)
    uc = _conv_module(glu, b, s, p["dw"], p["db"], p["ln_g"], p["ln_b"])
    at = _natten(q, kt, v, p["tab"], b, s)
    x = _merge(x, l, h, fm, uc, at, w["w_in"], p["b_g"], w["w_f"], w["w_c"], w["w_a"], w["w_o"])
    return _mlp(x, l, p["n2"], w["w1"], w["w2"])


def kernel(x_prompt, x_sample, norm1_g, w_in, b_gate, w_fourier, conv_dw, conv_db, conv_ln_g, conv_ln_b,
           w_conv_out, q_norm_g, k_norm_g, rpb, w_attn_out, w_out, norm2_g, w_mlp_in, w_mlp_out):
    depth = w_in.shape[0]
    w = dict(w_in=w_in.astype(BF16), w_f=w_fourier.astype(BF16), w_c=w_conv_out.astype(BF16),
             w_a=w_attn_out.astype(BF16), w_o=w_out.astype(BF16),
             w1=w_mlp_in.astype(BF16), w2=w_mlp_out.astype(BF16))
    layers = []
    for l in range(depth):
        layers.append(dict(
            n1=norm1_g[l][None, :], n2=norm2_g[l][None, :], b_g=b_gate[l][None, :],
            dw=conv_dw[l], db=conv_db[l][None, :], ln_g=conv_ln_g[l][None, :], ln_b=conv_ln_b[l][None, :],
            qg=q_norm_g[l][None, :], kg=k_norm_g[l][None, :],
            tab=_natten_bias_table(rpb[l]),
        ))

    def trunk(x):
        b, s, d = x.shape
        y = x.reshape(b * s, d)
        for l, p in enumerate(layers):
            y = _layer(y, b, s, l, w, p)
        return y.reshape(b, s, d)

    return (trunk(x_prompt), trunk(x_sample))
```

```python
import functools
import math

import numpy as np
import jax
import jax.numpy as jnp
from jax import lax
from jax.experimental import pallas as pl
from jax.experimental.pallas import tpu as pltpu

D_MODEL = 2048
GRID_W = 64
F_GROUPS = 4
F_GROUP_DIM = 128
F_W = F_GROUPS * F_GROUP_DIM
C_W = 512
CONV_K = 31
CONV_PAD = CONV_K // 2
N_HEADS = 8
HEAD_DIM = 128
A_W = N_HEADS * HEAD_DIM
WIN_R = 8
WIN_C = 16
N_BRANCH = 3
OFF_F = 0
OFF_CA = OFF_F + F_W
OFF_CG = OFF_CA + C_W
OFF_Q = OFF_CG + C_W
OFF_K = OFF_Q + A_W
OFF_V = OFF_K + A_W
OFF_G = OFF_V + A_W
IN_COLS = OFF_G + N_BRANCH * D_MODEL
D_FF = 4 * D_MODEL
EPS = 1e-6
NEG_INF = -1e30
LOG2E = math.log2(math.e)

BF16 = jnp.bfloat16
F32 = jnp.float32

V7X_VMEM_BYTES = 64 * 1024 * 1024
V7X_LANES = 128
V7X_SUBLANES = 8

TOKEN_TILE = 512
MERGE_COLS = 512
MLP_COLS = 1024
CONV_ROWS = 256
CONV_HALO = 16
CONV_CHUNK = 32
DFT_BLOCK_ROWS = 512
Q_ROWS = 4
K_GROUPS = 3
Q_TOK = Q_ROWS * GRID_W


def _vmem_limit(nbytes):
    return int(min(nbytes * 5 // 4 + (8 << 20), V7X_VMEM_BYTES - (4 << 20)))


def _rms_norm_bf16(x, gain):
    ms = jnp.mean(x * x, axis=-1, keepdims=True)
    return ((x * lax.rsqrt(ms + EPS)) * gain).astype(BF16)


def _in_proj_kernel(x_ref, g1_ref, w_ref, qg_ref, kg_ref, h_ref, f_ref, glu_ref, q_ref, kt_ref, v_ref, zf_ref,
                    *, s2):
    h = _rms_norm_bf16(x_ref[...], g1_ref[...])
    h_ref[...] = h

    def proj(c0, n):
        return jnp.dot(h, w_ref[:, c0:c0 + n], preferred_element_type=F32)

    zf = proj(OFF_F, F_W)
    n_lt = F_W // V7X_LANES
    for c in range(n_lt):
        zf_ref[c] = zf[:, c * V7X_LANES:(c + 1) * V7X_LANES]
    rows = zf.shape[0] // s2
    for n2 in range(s2):
        for c in range(n_lt):
            piece = zf_ref[c, pl.ds(n2, rows, stride=s2), :]
            lane0 = n2 * F_W + c * V7X_LANES
            f_ref[:, lane0:lane0 + V7X_LANES] = piece.astype(BF16)

    a = proj(OFF_CA, C_W)
    g = proj(OFF_CG, C_W)
    glu_ref[...] = a * jax.nn.sigmoid(g)

    def head_norm(z, gain, scale, store):
        for hd in range(N_HEADS):
            zh = z[:, hd * HEAD_DIM:(hd + 1) * HEAD_DIM]
            ms = jnp.mean(zh * zh, axis=-1, keepdims=True)
            y = (zh * lax.rsqrt(ms + EPS)) * gain
            if scale != 1.0:
                y = y * scale
            store(hd, y)

    def store_q(hd, y):
        q_ref[:, hd * HEAD_DIM:(hd + 1) * HEAD_DIM] = y.astype(BF16)

    def store_k_transposed(hd, y):
        yt = jnp.transpose(y).astype(BF16)
        for blk in range(y.shape[0] // Q_TOK):
            kt_ref[blk, hd] = yt[:, blk * Q_TOK:(blk + 1) * Q_TOK]

    head_norm(proj(OFF_Q, A_W), qg_ref[...], LOG2E / math.sqrt(HEAD_DIM), store_q)
    head_norm(proj(OFF_K, A_W), kg_ref[...], 1.0, store_k_transposed)
    v_ref[...] = proj(OFF_V, A_W).astype(BF16)


def _in_proj(x, layer, g1, w_in, qg, kg, s2):
    t = x.shape[0]
    tm = TOKEN_TILE
    row = lambda i: (i, 0)
    fixed = lambda i: (0, 0)
    vmem = (2 * tm * D_MODEL * 4 + D_MODEL * OFF_G * 2
            + 2 * tm * (D_MODEL * 2 + F_W * 2 + C_W * 4 + 3 * A_W * 2) + 2 * tm * A_W * 4 + tm * F_W * 4)
    return pl.pallas_call(
        functools.partial(_in_proj_kernel, s2=s2),
        grid=(t // tm,),
        in_specs=[
            pl.BlockSpec((tm, D_MODEL), row),
            pl.BlockSpec((1, D_MODEL), fixed),
            pl.BlockSpec((None, D_MODEL, OFF_G), lambda i: (layer, 0, 0), pipeline_mode=pl.Buffered(1)),
            pl.BlockSpec((1, HEAD_DIM), fixed),
            pl.BlockSpec((1, HEAD_DIM), fixed),
        ],
        out_specs=[
            pl.BlockSpec((tm, D_MODEL), row),
            pl.BlockSpec((tm // s2, s2 * F_W), row),
            pl.BlockSpec((tm, C_W), row),
            pl.BlockSpec((tm, A_W), row),
            pl.BlockSpec((tm // Q_TOK, N_HEADS, HEAD_DIM, Q_TOK), lambda i: (i, 0, 0, 0)),
            pl.BlockSpec((tm, A_W), row),
        ],
        out_shape=[
            jax.ShapeDtypeStruct((t, D_MODEL), BF16),
            jax.ShapeDtypeStruct((t // s2, s2 * F_W), BF16),
            jax.ShapeDtypeStruct((t, C_W), F32),
            jax.ShapeDtypeStruct((t, A_W), BF16),
            jax.ShapeDtypeStruct((t // Q_TOK, N_HEADS, HEAD_DIM, Q_TOK), BF16),
            jax.ShapeDtypeStruct((t, A_W), BF16),
        ],
        scratch_shapes=[pltpu.VMEM((F_W // V7X_LANES, tm, V7X_LANES), F32)],
        compiler_params=pltpu.CompilerParams(
            dimension_semantics=("arbitrary",), vmem_limit_bytes=_vmem_limit(vmem)),
        name="in_proj",
    )(x, g1, w_in, qg, kg)


def _dft_factors(s):
    s1 = 256 if s >= 8192 else 128
    return s1, s // s1


@functools.lru_cache(maxsize=None)
def _dft_tables(s):
    s1, s2 = _dft_factors(s)
    n1 = np.arange(s1)
    ang1 = 2.0 * np.pi * np.outer(n1, n1) / s1
    f1 = np.concatenate([np.cos(ang1), -np.sin(ang1)], axis=0) / np.sqrt(s1)
    k1 = np.arange(s1)[:, None, None]
    k2 = np.arange(s2)[None, :, None]
    n2 = np.arange(s2)[None, None, :]
    ang2 = 2.0 * np.pi * ((n2 * (k1 + s1 * k2)) % s) / s
    gr = np.cos(ang2) / np.sqrt(s2)
    gi = -np.sin(ang2) / np.sqrt(s2)
    g2 = np.concatenate([np.concatenate([gr, -gi], axis=2),
                         np.concatenate([gi, gr], axis=2)], axis=1)
    c = np.arange(F_GROUP_DIM)
    ang3 = 2.0 * np.pi * np.outer(c, c) / F_GROUP_DIM
    cs = np.concatenate([np.cos(ang3), np.sin(ang3)], axis=0) / np.sqrt(F_GROUP_DIM)
    kb = DFT_BLOCK_ROWS // s2
    perm = np.zeros((kb * s2, kb * s2), np.float32)
    for a in range(kb):
        for r in range(s2):
            perm[r * kb + a, a * s2 + r] = 1.0
    as_bf16 = lambda a: jnp.asarray(a.astype(np.float32)).astype(BF16)
    return as_bf16(f1), as_bf16(g2), as_bf16(cs), as_bf16(perm)


def _dft_seq1_kernel(f1_ref, x_ref, a_ref):
    a_ref[0] = jnp.dot(f1_ref[...], x_ref[0], preferred_element_type=F32).astype(BF16)


def _dft_seq2_kernel(g2_ref, cs_ref, perm_ref, a_ref, o_ref, *, kb, s2):
    a2 = jnp.concatenate([a_ref[0, 0], a_ref[0, 1]], axis=1)
    w = jnp.einsum("kmn,knc->kmc", g2_ref[...], a2, preferred_element_type=F32)
    wr = w[:, :s2, :].astype(BF16)
    wi = w[:, s2:, :].astype(BF16)
    groups = []
    for g in range(F_GROUPS):
        gs = slice(g * F_GROUP_DIM, (g + 1) * F_GROUP_DIM)
        lhs = jnp.concatenate([wr[:, :, gs], wi[:, :, gs]], axis=-1).reshape(kb * s2, 2 * F_GROUP_DIM)
        groups.append(jnp.dot(lhs, cs_ref[...], preferred_element_type=F32).astype(BF16))
    og = jnp.concatenate(groups, axis=-1)
    out = jnp.dot(perm_ref[...], og, preferred_element_type=F32).astype(BF16)
    o_ref[0] = out.reshape(s2, kb, F_W)


def _fourier_mix(x3, b, s):
    s1, s2 = _dft_factors(s)
    f1, g2, cs, perm = _dft_tables(s)
    nb = min(s2, 8)
    x = x3.reshape(b, s1, s2 * F_W)
    a = pl.pallas_call(
        _dft_seq1_kernel,
        grid=(b, s2 // nb),
        in_specs=[
            pl.BlockSpec((2 * s1, s1), lambda i, j: (0, 0)),
            pl.BlockSpec((1, s1, nb * F_W), lambda i, j: (i, 0, j)),
        ],
        out_specs=pl.BlockSpec((1, 2 * s1, nb * F_W), lambda i, j: (i, 0, j)),
        out_shape=jax.ShapeDtypeStruct((b, 2 * s1, s2 * F_W), BF16),
        compiler_params=pltpu.CompilerParams(
            dimension_semantics=("arbitrary", "arbitrary"),
            vmem_limit_bytes=_vmem_limit(2 * 3 * s1 * nb * F_W * 2 + 2 * s1 * nb * F_W * 4)),
        name="dft_seq1",
    )(f1, x)
    a = a.reshape(b, 2, s1, s2, F_W)
    kb = DFT_BLOCK_ROWS // s2
    out = pl.pallas_call(
        functools.partial(_dft_seq2_kernel, kb=kb, s2=s2),
        grid=(b, s1 // kb),
        in_specs=[
            pl.BlockSpec((kb, 2 * s2, 2 * s2), lambda i, j: (j, 0, 0)),
            pl.BlockSpec((2 * F_GROUP_DIM, F_GROUP_DIM), lambda i, j: (0, 0)),
            pl.BlockSpec((kb * s2, kb * s2), lambda i, j: (0, 0)),
            pl.BlockSpec((1, 2, kb, s2, F_W), lambda i, j: (i, 0, j, 0, 0)),
        ],
        out_specs=pl.BlockSpec((1, s2, kb, F_W), lambda i, j: (i, 0, j, 0)),
        out_shape=jax.ShapeDtypeStruct((b, s2, s1, F_W), BF16),
        compiler_params=pltpu.CompilerParams(
            dimension_semantics=("arbitrary", "arbitrary"),
            vmem_limit_bytes=_vmem_limit(2 * 3 * kb * s2 * F_W * 2 + 4 * kb * s2 * F_W * 4
                                         + 2 * (kb * s2) ** 2 * 2)),
        name="dft_seq2",
    )(g2, cs, perm, a)
    return out.reshape(b * s, F_W)


def _conv_kernel(prev_ref, cur_ref, next_ref, dw_ref, db_ref, lg_ref, lb_ref, o_ref, pad_ref, conv_ref):
    i = pl.program_id(1)
    last = pl.num_programs(1) - 1
    prev = jnp.where(i > 0, prev_ref[0, 0], 0.0)
    nxt = jnp.where(i < last, next_ref[0, 0], 0.0)
    n_lt = C_W // V7X_LANES
    lane_tiles = [slice(c * V7X_LANES, (c + 1) * V7X_LANES) for c in range(n_lt)]
    for c, ls in enumerate(lane_tiles):
        pad_ref[c, 0:CONV_HALO, :] = prev[:, ls]
        pad_ref[c, CONV_HALO:CONV_HALO + CONV_ROWS, :] = cur_ref[0, :, ls]
        pad_ref[c, CONV_HALO + CONV_ROWS:, :] = nxt[:, ls]
    first = CONV_HALO - CONV_PAD
    n_sub = CONV_CHUNK // V7X_SUBLANES
    for c, ls in enumerate(lane_tiles):
        w = [dw_ref[t, :, ls] for t in range(CONV_K)]
        bias = jnp.broadcast_to(db_ref[:, ls], (V7X_SUBLANES, V7X_LANES))

        def chunk_body(ch, carry, c=c, w=w, bias=bias):
            r0 = pl.multiple_of(ch * CONV_CHUNK, CONV_CHUNK)
            acc = [bias for _ in range(n_sub)]
            for t in range(CONV_K):
                for k in range(n_sub):
                    row = r0 + (k * V7X_SUBLANES + first + t)
                    acc[k] = acc[k] + w[t] * pad_ref[c, pl.ds(row, V7X_SUBLANES), :]
            conv_ref[c, pl.ds(r0, CONV_CHUNK), :] = jnp.concatenate(acc, axis=0)
            return carry

        lax.fori_loop(0, CONV_ROWS // CONV_CHUNK, chunk_body, 0)

    for ch in range(CONV_ROWS // CONV_CHUNK):
        r0 = ch * CONV_CHUNK
        accs = [conv_ref[c, r0:r0 + CONV_CHUNK, :] for c in range(n_lt)]
        mu = sum(accs).sum(axis=-1, keepdims=True) * (1.0 / C_W)
        xcs = [a - mu for a in accs]
        var = sum(x * x for x in xcs).sum(axis=-1, keepdims=True) * (1.0 / C_W)
        rstd = lax.rsqrt(var + EPS)
        for c, ls in enumerate(lane_tiles):
            y = (xcs[c] * rstd) * lg_ref[:, ls] + lb_ref[:, ls]
            o_ref[0, r0:r0 + CONV_CHUNK, ls] = (y * jax.nn.sigmoid(y)).astype(BF16)


def _conv_module(glu, b, s, dw, db, lg, lb):
    nblk = s // CONV_ROWS
    per = CONV_ROWS // CONV_HALO
    x3 = glu.reshape(b, s, C_W)
    x4 = glu.reshape(b, s // CONV_HALO, CONV_HALO, C_W)
    dwb = jnp.broadcast_to(dw[:, None, :], (CONV_K, V7X_SUBLANES, C_W))
    vec = lambda i, j: (0, 0)
    out = pl.pallas_call(
        _conv_kernel,
        grid=(b, nblk),
        in_specs=[
            pl.BlockSpec((1, 1, CONV_HALO, C_W), lambda i, j: (i, jnp.maximum(j * per - 1, 0), 0, 0)),
            pl.BlockSpec((1, CONV_ROWS, C_W), lambda i, j: (i, j, 0)),
            pl.BlockSpec((1, 1, CONV_HALO, C_W),
                         lambda i, j: (i, jnp.minimum((j + 1) * per, s // CONV_HALO - 1), 0, 0)),
            pl.BlockSpec((CONV_K, V7X_SUBLANES, C_W), lambda i, j: (0, 0, 0)),
            pl.BlockSpec((1, C_W), vec),
            pl.BlockSpec((1, C_W), vec),
            pl.BlockSpec((1, C_W), vec),
        ],
        out_specs=pl.BlockSpec((1, CONV_ROWS, C_W), lambda i, j: (i, j, 0)),
        out_shape=jax.ShapeDtypeStruct((b, s, C_W), BF16),
        scratch_shapes=[pltpu.VMEM((C_W // V7X_LANES, CONV_ROWS + 2 * CONV_HALO, V7X_LANES), F32),
                        pltpu.VMEM((C_W // V7X_LANES, CONV_ROWS, V7X_LANES), F32)],
        compiler_params=pltpu.CompilerParams(
            dimension_semantics=("arbitrary", "arbitrary"),
            vmem_limit_bytes=_vmem_limit(4 * CONV_ROWS * C_W * 4 + 2 * CONV_K * V7X_SUBLANES * C_W * 4)),
        name="conv",
    )(x4, x3, x4, dwb, db, lg, lb)
    return out.reshape(b * s, C_W)


@functools.lru_cache(maxsize=None)
def _natten_index_tables():
    n_key_rows = K_GROUPS * Q_ROWS
    n_dr, n_dc = 2 * WIN_R - 1, 2 * WIN_C - 1
    i = np.arange(Q_ROWS)[:, None]
    rr = np.arange(n_key_rows)[None, :]
    lo = (np.zeros_like(i), i, np.full_like(i, n_key_rows - WIN_R))
    dr_off = (WIN_R - 1, WIN_R - 1 - Q_ROWS, WIN_R - 1 - 2 * Q_ROWS)
    row_sel = np.zeros((3, Q_ROWS, n_key_rows, n_dr), np.float32)
    for p in range(3):
        row_ok = (rr >= lo[p]) & (rr < lo[p] + WIN_R)
        dr = rr - i + dr_off[p]
        for a in range(Q_ROWS):
            for r in range(n_key_rows):
                if row_ok[a, r]:
                    row_sel[p, a, r, dr[a, r]] = 1.0
    qc = np.arange(GRID_W)[:, None]
    kc = np.arange(GRID_W)[None, :]
    start = np.clip(qc - WIN_C // 2, 0, GRID_W - WIN_C)
    col_ok = (kc >= start) & (kc < start + WIN_C)
    dc = kc - qc + (WIN_C - 1)
    col_sel = np.zeros((n_dc, GRID_W, GRID_W), np.float32)
    for a in range(GRID_W):
        for c in range(GRID_W):
            if col_ok[a, c]:
                col_sel[dc[a, c], a, c] = 1.0
    ok = (row_sel.sum(-1) > 0)[:, :, None, :, None] & col_ok[None, None, :, None, :]
    return row_sel, col_sel, ok.reshape(3, Q_TOK, n_key_rows * GRID_W)


def _natten_bias_table(rpb):
    row_sel, col_sel, ok = _natten_index_tables()
    tab = jnp.einsum("pird,hdc,cqk->phiqrk", row_sel, rpb, col_sel, precision=lax.Precision.HIGHEST)
    tab = tab.reshape(3, N_HEADS, Q_TOK, K_GROUPS * Q_TOK)
    return jnp.where(ok[:, None], tab * LOG2E, NEG_INF)


def _natten_kernel(q_ref, kt0_ref, kt1_ref, kt2_ref, v0_ref, v1_ref, v2_ref, tab_ref, o_ref):
    kt_refs = (kt0_ref, kt1_ref, kt2_ref)
    v_refs = (v0_ref, v1_ref, v2_ref)
    j = pl.program_id(1)
    pat = jnp.where(j == 0, 0, jnp.where(j == pl.num_programs(1) - 1, 2, 1))

    def scores(hd):
        qh = q_ref[:, hd * HEAD_DIM:(hd + 1) * HEAD_DIM]
        return [jnp.dot(qh, kt_refs[g][hd], preferred_element_type=F32)
                + tab_ref[pat, hd, :, g * Q_TOK:(g + 1) * Q_TOK] for g in range(K_GROUPS)]

    def fold_lanes(parts, op):
        acc = parts[0]
        for part in parts[1:]:
            acc = op(acc, part)
        tiles = [acc[:, c * V7X_LANES:(c + 1) * V7X_LANES] for c in range(acc.shape[1] // V7X_LANES)]
        out = tiles[0]
        for tile in tiles[1:]:
            out = op(out, tile)
        return out

    sc_next = scores(0)
    for hd in range(N_HEADS):
        hs = slice(hd * HEAD_DIM, (hd + 1) * HEAD_DIM)
        sc = sc_next
        if hd + 1 < N_HEADS:
            sc_next = scores(hd + 1)
        m = fold_lanes(sc, jnp.maximum).max(axis=-1, keepdims=True)
        p = [jnp.exp2(s - m) for s in sc]
        denom = fold_lanes(p, jnp.add).sum(axis=-1, keepdims=True)
        o = jnp.dot(p[0].astype(BF16), v_refs[0][:, hs], preferred_element_type=F32)
        for g in range(1, K_GROUPS):
            o += jnp.dot(p[g].astype(BF16), v_refs[g][:, hs], preferred_element_type=F32)
        o_ref[:, hs] = (o * (1.0 / denom)).astype(BF16)


def _natten(q, kt, v, tab, b, s):
    ng = s // Q_TOK
    assert ng >= K_GROUPS

    def q_map(i, j):
        return (i * ng + j, 0)

    def window_block(i, j, off):
        return i * ng + jnp.clip(j - 1, 0, ng - K_GROUPS) + off

    blk = (Q_TOK, A_W)
    tab_shape = (3, N_HEADS, Q_TOK, K_GROUPS * Q_TOK)
    vmem = 2 * (2 + 2 * K_GROUPS) * Q_TOK * A_W * 2 + math.prod(tab_shape) * 4
    return pl.pallas_call(
        _natten_kernel,
        grid=(b, ng),
        in_specs=[pl.BlockSpec(blk, q_map)]
        + [pl.BlockSpec((None, N_HEADS, HEAD_DIM, Q_TOK), lambda i, j, g=g: (window_block(i, j, g), 0, 0, 0))
           for g in range(K_GROUPS)]
        + [pl.BlockSpec(blk, lambda i, j, g=g: (window_block(i, j, g), 0)) for g in range(K_GROUPS)]
        + [pl.BlockSpec(tab_shape, lambda i, j: (0, 0, 0, 0), pipeline_mode=pl.Buffered(1))],
        out_specs=pl.BlockSpec(blk, q_map),
        out_shape=jax.ShapeDtypeStruct((b * s, A_W), BF16),
        compiler_params=pltpu.CompilerParams(
            dimension_semantics=("arbitrary", "arbitrary"), vmem_limit_bytes=_vmem_limit(vmem)),
        name="natten",
    )(q, kt, kt, kt, v, v, v, tab)


def _merge_kernel(x_ref, h_ref, fm_ref, uc_ref, at_ref, wg0_ref, wg1_ref, wg2_ref,
                  bg0_ref, bg1_ref, bg2_ref, wf_ref, wc_ref, wa_ref, wo_ref, o_ref):
    def gate(w_ref, b_ref):
        return jax.nn.sigmoid(jnp.dot(h_ref[...], w_ref[...], preferred_element_type=F32) + b_ref[...])

    def step(acc_ref):
        merged = gate(wg0_ref, bg0_ref) * jnp.dot(fm_ref[...], wf_ref[...], preferred_element_type=F32)
        merged += gate(wg1_ref, bg1_ref) * jnp.dot(uc_ref[...], wc_ref[...], preferred_element_type=F32)
        merged += gate(wg2_ref, bg2_ref) * jnp.dot(at_ref[...], wa_ref[...], preferred_element_type=F32)
        o_ref[...] = acc_ref[...] + jnp.dot(merged.astype(BF16), wo_ref[...], preferred_element_type=F32)

    first = pl.program_id(1) == 0
    pl.when(first)(functools.partial(step, x_ref))
    pl.when(jnp.logical_not(first))(functools.partial(step, o_ref))


def _merge(x, layer, h, fm, uc, at, w_gate, b_gate, w_f, w_c, w_a, w_o):
    t = x.shape[0]
    tm, tn = TOKEN_TILE, MERGE_COLS
    nj = D_MODEL // tn
    row = lambda i, j: (i, 0)
    col = lambda i, j: (layer, j, 0, 0)
    gate_specs = [pl.BlockSpec((None, None, None, D_MODEL, tn), lambda i, j, br=br: (layer, br, j, 0, 0))
                  for br in range(N_BRANCH)]
    bias_specs = [pl.BlockSpec((1, tn), lambda i, j, br=br: (0, br * nj + j)) for br in range(N_BRANCH)]
    vmem = (4 * tm * D_MODEL * 4 + 2 * tm * (D_MODEL + F_W + C_W + A_W) * 2
            + 2 * (3 * D_MODEL + F_W + C_W + A_W + D_MODEL) * tn * 2 + 6 * tm * tn * 4)
    return pl.pallas_call(
        _merge_kernel,
        grid=(t // tm, nj),
        in_specs=[
            pl.BlockSpec((tm, D_MODEL), row),
            pl.BlockSpec((tm, D_MODEL), row),
            pl.BlockSpec((tm, F_W), row),
            pl.BlockSpec((tm, C_W), row),
            pl.BlockSpec((tm, A_W), row),
            *gate_specs,
            *bias_specs,
            pl.BlockSpec((None, None, F_W, tn), col),
            pl.BlockSpec((None, None, C_W, tn), col),
            pl.BlockSpec((None, None, A_W, tn), col),
            pl.BlockSpec((None, tn, D_MODEL), lambda i, j: (layer, j, 0)),
        ],
        out_specs=pl.BlockSpec((tm, D_MODEL), row),
        out_shape=jax.ShapeDtypeStruct((t, D_MODEL), F32),
        compiler_params=pltpu.CompilerParams(
            dimension_semantics=("arbitrary", "arbitrary"), vmem_limit_bytes=_vmem_limit(vmem)),
        name="merge",
    )(x, h, fm, uc, at, w_gate, w_gate, w_gate, b_gate, b_gate, b_gate, w_f, w_c, w_a, w_o)


def _mlp_kernel(x_ref, g2_ref, w1_ref, w2_ref, o_ref, h_ref):
    def step(acc_ref):
        a = jnp.maximum(jnp.dot(h_ref[...], w1_ref[...], preferred_element_type=F32), 0.0)
        o_ref[...] = acc_ref[...] + jnp.dot((a * a).astype(BF16), w2_ref[...], preferred_element_type=F32)

    @pl.when(pl.program_id(1) == 0)
    def _():
        h_ref[...] = _rms_norm_bf16(x_ref[...], g2_ref[...])
        step(x_ref)

    pl.when(pl.program_id(1) > 0)(functools.partial(step, o_ref))


def _mlp(x, layer, g2, w1, w2):
    t = x.shape[0]
    tm, tf = TOKEN_TILE, MLP_COLS
    row = lambda i, j: (i, 0)
    vmem = 4 * tm * D_MODEL * 4 + tm * D_MODEL * 2 + 2 * 2 * D_MODEL * tf * 2 + 2 * tm * tf * 4
    return pl.pallas_call(
        _mlp_kernel,
        grid=(t // tm, D_FF // tf),
        in_specs=[
            pl.BlockSpec((tm, D_MODEL), row),
            pl.BlockSpec((1, D_MODEL), lambda i, j: (0, 0)),
            pl.BlockSpec((None, None, D_MODEL, tf), lambda i, j: (layer, j, 0, 0)),
            pl.BlockSpec((None, tf, D_MODEL), lambda i, j: (layer, j, 0)),
        ],
        out_specs=pl.BlockSpec((tm, D_MODEL), row),
        out_shape=jax.ShapeDtypeStruct((t, D_MODEL), F32),
        scratch_shapes=[pltpu.VMEM((tm, D_MODEL), BF16)],
        compiler_params=pltpu.CompilerParams(
            dimension_semantics=("arbitrary", "arbitrary"), vmem_limit_bytes=_vmem_limit(vmem)),
        name="mlp",
    )(x, g2, w1, w2)


def _layer(x, b, s, l, w, p):
    _, s2 = _dft_factors(s)
    h, x3, glu, q, kt, v = _in_proj(x, l, p["n1"], w["w_in"], p["qg"], p["kg"], s2)
    fm = _fourier_mix(x3, b, s)
    uc = _conv_module(glu, b, s, p["dw"], p["db"], p["ln_g"], p["ln_b"])
    at = _natten(q, kt, v, p["tab"], b, s)
    x = _merge(x, l, h, fm, uc, at, w["w_gate"], p["b_g"], w["w_f"], w["w_c"], w["w_a"], w["w_o"])
    return _mlp(x, l, p["n2"], w["w1"], w["w2"])


def kernel(x_prompt, x_sample, norm1_g, w_in, b_gate, w_fourier, conv_dw, conv_db, conv_ln_g, conv_ln_b,
           w_conv_out, q_norm_g, k_norm_g, rpb, w_attn_out, w_out, norm2_g, w_mlp_in, w_mlp_out):
    depth = w_in.shape[0]

    def column_blocks(wt, tn):
        d, k, n = wt.shape
        return jnp.transpose(wt.reshape(d, k, n // tn, tn), (0, 2, 1, 3)).astype(BF16)

    nj = D_MODEL // MERGE_COLS
    w = dict(w_in=w_in[:, :, :OFF_G].astype(BF16),
             w_gate=column_blocks(w_in[:, :, OFF_G:], MERGE_COLS).reshape(
                 depth, N_BRANCH, nj, D_MODEL, MERGE_COLS),
             w_f=column_blocks(w_fourier, MERGE_COLS), w_c=column_blocks(w_conv_out, MERGE_COLS),
             w_a=column_blocks(w_attn_out, MERGE_COLS), w_o=w_out.astype(BF16),
             w1=column_blocks(w_mlp_in, MLP_COLS), w2=w_mlp_out.astype(BF16))
    layers = []
    for l in range(depth):
        layers.append(dict(
            n1=norm1_g[l][None, :], n2=norm2_g[l][None, :], b_g=b_gate[l][None, :],
            dw=conv_dw[l], db=conv_db[l][None, :], ln_g=conv_ln_g[l][None, :], ln_b=conv_ln_b[l][None, :],
            qg=q_norm_g[l][None, :], kg=k_norm_g[l][None, :],
            tab=_natten_bias_table(rpb[l]),
        ))

    def trunk(x):
        b, s, d = x.shape
        y = x.reshape(b * s, d)
        for l, p in enumerate(layers):
            y = _layer(y, b, s, l, w, p)
        return y.reshape(b, s, d)

    return (trunk(x_prompt), trunk(x_sample))
```

```python
import functools
import math

import numpy as np
import jax
import jax.numpy as jnp
from jax import lax
from jax.experimental import pallas as pl
from jax.experimental.pallas import tpu as pltpu

D_MODEL = 2048
GRID_W = 64
F_GROUPS = 4
F_GROUP_DIM = 128
F_W = F_GROUPS * F_GROUP_DIM
C_W = 512
CONV_K = 31
CONV_PAD = CONV_K // 2
N_HEADS = 8
HEAD_DIM = 128
A_W = N_HEADS * HEAD_DIM
WIN_R = 8
WIN_C = 16
N_BRANCH = 3
OFF_F = 0
OFF_CA = OFF_F + F_W
OFF_CG = OFF_CA + C_W
OFF_Q = OFF_CG + C_W
OFF_K = OFF_Q + A_W
OFF_V = OFF_K + A_W
OFF_G = OFF_V + A_W
IN_COLS = OFF_G + N_BRANCH * D_MODEL
D_FF = 4 * D_MODEL
EPS = 1e-6
NEG_INF = -1e30
LOG2E = math.log2(math.e)

BF16 = jnp.bfloat16
F32 = jnp.float32

V7X_VMEM_BYTES = 64 * 1024 * 1024
V7X_LANES = 128
V7X_SUBLANES = 8

TOKEN_TILE = 512
MERGE_COLS = 512
MLP_COLS = 2048
CONV_ROWS = 512
CONV_HALO = 16
CONV_CHUNK = 32
DFT_BLOCK_ROWS = 512
Q_ROWS = 4
K_GROUPS = 3
Q_TOK = Q_ROWS * GRID_W


def _vmem_limit(nbytes):
    return int(min(nbytes * 5 // 4 + (8 << 20), V7X_VMEM_BYTES - (4 << 20)))


def _rms_norm_bf16(x, gain):
    ms = jnp.mean(x * x, axis=-1, keepdims=True)
    return ((x * lax.rsqrt(ms + EPS)) * gain).astype(BF16)


def _in_proj_kernel(x_ref, g1_ref, w_ref, qg_ref, kg_ref, h_ref, f_ref, glu_ref, q_ref, kt_ref, v_ref, zf_ref,
                    *, s2):
    h = _rms_norm_bf16(x_ref[...], g1_ref[...])
    h_ref[...] = h

    def proj(c0, n):
        return jnp.dot(h, w_ref[:, c0:c0 + n], preferred_element_type=F32)

    zf = proj(OFF_F, F_W)
    n_lt = F_W // V7X_LANES
    for c in range(n_lt):
        zf_ref[c] = zf[:, c * V7X_LANES:(c + 1) * V7X_LANES]
    rows = zf.shape[0] // s2
    for n2 in range(s2):
        for c in range(n_lt):
            piece = zf_ref[c, pl.ds(n2, rows, stride=s2), :]
            lane0 = n2 * F_W + c * V7X_LANES
            f_ref[:, lane0:lane0 + V7X_LANES] = piece.astype(BF16)

    a = proj(OFF_CA, C_W)
    g = proj(OFF_CG, C_W)
    glu_ref[...] = a * jax.nn.sigmoid(g)

    def head_norm(z, gain, scale, store):
        for hd in range(N_HEADS):
            zh = z[:, hd * HEAD_DIM:(hd + 1) * HEAD_DIM]
            ms = jnp.mean(zh * zh, axis=-1, keepdims=True)
            y = (zh * lax.rsqrt(ms + EPS)) * gain
            if scale != 1.0:
                y = y * scale
            store(hd, y)

    def store_q(hd, y):
        q_ref[:, hd * HEAD_DIM:(hd + 1) * HEAD_DIM] = y.astype(BF16)

    def store_k_transposed(hd, y):
        yt = jnp.transpose(y).astype(BF16)
        for blk in range(y.shape[0] // Q_TOK):
            kt_ref[blk, hd] = yt[:, blk * Q_TOK:(blk + 1) * Q_TOK]

    head_norm(proj(OFF_Q, A_W), qg_ref[...], LOG2E / math.sqrt(HEAD_DIM), store_q)
    head_norm(proj(OFF_K, A_W), kg_ref[...], 1.0, store_k_transposed)
    v_ref[...] = proj(OFF_V, A_W).astype(BF16)


def _in_proj(x, layer, g1, w_in, qg, kg, s2):
    t = x.shape[0]
    tm = TOKEN_TILE
    row = lambda i: (i, 0)
    fixed = lambda i: (0, 0)
    vmem = (2 * tm * D_MODEL * 4 + D_MODEL * OFF_G * 2
            + 2 * tm * (D_MODEL * 2 + F_W * 2 + C_W * 4 + 3 * A_W * 2) + 2 * tm * A_W * 4 + tm * F_W * 4)
    return pl.pallas_call(
        functools.partial(_in_proj_kernel, s2=s2),
        grid=(t // tm,),
        in_specs=[
            pl.BlockSpec((tm, D_MODEL), row),
            pl.BlockSpec((1, D_MODEL), fixed),
            pl.BlockSpec((None, D_MODEL, OFF_G), lambda i: (layer, 0, 0), pipeline_mode=pl.Buffered(1)),
            pl.BlockSpec((1, HEAD_DIM), fixed),
            pl.BlockSpec((1, HEAD_DIM), fixed),
        ],
        out_specs=[
            pl.BlockSpec((tm, D_MODEL), row),
            pl.BlockSpec((tm // s2, s2 * F_W), row),
            pl.BlockSpec((tm, C_W), row),
            pl.BlockSpec((tm, A_W), row),
            pl.BlockSpec((tm // Q_TOK, N_HEADS, HEAD_DIM, Q_TOK), lambda i: (i, 0, 0, 0)),
            pl.BlockSpec((tm, A_W), row),
        ],
        out_shape=[
            jax.ShapeDtypeStruct((t, D_MODEL), BF16),
            jax.ShapeDtypeStruct((t // s2, s2 * F_W), BF16),
            jax.ShapeDtypeStruct((t, C_W), F32),
            jax.ShapeDtypeStruct((t, A_W), BF16),
            jax.ShapeDtypeStruct((t // Q_TOK, N_HEADS, HEAD_DIM, Q_TOK), BF16),
            jax.ShapeDtypeStruct((t, A_W), BF16),
        ],
        scratch_shapes=[pltpu.VMEM((F_W // V7X_LANES, tm, V7X_LANES), F32)],
        compiler_params=pltpu.CompilerParams(
            dimension_semantics=("arbitrary",), vmem_limit_bytes=_vmem_limit(vmem)),
        name="in_proj",
    )(x, g1, w_in, qg, kg)


def _dft_factors(s):
    s1 = 256 if s >= 8192 else 128
    return s1, s // s1


@functools.lru_cache(maxsize=None)
def _dft_tables(s):
    s1, s2 = _dft_factors(s)
    n1 = np.arange(s1)
    ang1 = 2.0 * np.pi * np.outer(n1, n1) / s1
    f1 = np.concatenate([np.cos(ang1), -np.sin(ang1)], axis=0) / np.sqrt(s1)
    k1 = np.arange(s1)[:, None, None]
    k2 = np.arange(s2)[None, :, None]
    n2 = np.arange(s2)[None, None, :]
    ang2 = 2.0 * np.pi * ((n2 * (k1 + s1 * k2)) % s) / s
    gr = np.cos(ang2) / np.sqrt(s2)
    gi = -np.sin(ang2) / np.sqrt(s2)
    g2 = np.concatenate([np.concatenate([gr, -gi], axis=2),
                         np.concatenate([gi, gr], axis=2)], axis=1)
    c = np.arange(F_GROUP_DIM)
    ang3 = 2.0 * np.pi * np.outer(c, c) / F_GROUP_DIM
    cs = np.concatenate([np.cos(ang3), np.sin(ang3)], axis=0) / np.sqrt(F_GROUP_DIM)
    kb = DFT_BLOCK_ROWS // s2
    perm = np.zeros((kb * s2, kb * s2), np.float32)
    for a in range(kb):
        for r in range(s2):
            perm[r * kb + a, a * s2 + r] = 1.0
    as_bf16 = lambda a: jnp.asarray(a.astype(np.float32)).astype(BF16)
    return as_bf16(f1), as_bf16(g2), as_bf16(cs), as_bf16(perm)


def _dft_seq1_kernel(f1_ref, x_ref, a_ref):
    a_ref[0] = jnp.dot(f1_ref[...], x_ref[0], preferred_element_type=F32).astype(BF16)


def _dft_seq2_kernel(g2_ref, cs_ref, perm_ref, a_ref, o_ref, *, kb, s2):
    a2 = jnp.concatenate([a_ref[0, 0], a_ref[0, 1]], axis=1)
    w = jnp.einsum("kmn,knc->kmc", g2_ref[...], a2, preferred_element_type=F32)
    wr = w[:, :s2, :].astype(BF16)
    wi = w[:, s2:, :].astype(BF16)
    groups = []
    for g in range(F_GROUPS):
        gs = slice(g * F_GROUP_DIM, (g + 1) * F_GROUP_DIM)
        lhs = jnp.concatenate([wr[:, :, gs], wi[:, :, gs]], axis=-1).reshape(kb * s2, 2 * F_GROUP_DIM)
        groups.append(jnp.dot(lhs, cs_ref[...], preferred_element_type=F32).astype(BF16))
    og = jnp.concatenate(groups, axis=-1)
    out = jnp.dot(perm_ref[...], og, preferred_element_type=F32).astype(BF16)
    o_ref[0] = out.reshape(s2, kb, F_W)


def _fourier_mix(x3, b, s):
    s1, s2 = _dft_factors(s)
    f1, g2, cs, perm = _dft_tables(s)
    nb = min(s2, 8)
    x = x3.reshape(b, s1, s2 * F_W)
    a = pl.pallas_call(
        _dft_seq1_kernel,
        grid=(b, s2 // nb),
        in_specs=[
            pl.BlockSpec((2 * s1, s1), lambda i, j: (0, 0)),
            pl.BlockSpec((1, s1, nb * F_W), lambda i, j: (i, 0, j)),
        ],
        out_specs=pl.BlockSpec((1, 2 * s1, nb * F_W), lambda i, j: (i, 0, j)),
        out_shape=jax.ShapeDtypeStruct((b, 2 * s1, s2 * F_W), BF16),
        compiler_params=pltpu.CompilerParams(
            dimension_semantics=("arbitrary", "arbitrary"),
            vmem_limit_bytes=_vmem_limit(2 * 3 * s1 * nb * F_W * 2 + 2 * s1 * nb * F_W * 4)),
        name="dft_seq1",
    )(f1, x)
    a = a.reshape(b, 2, s1, s2, F_W)
    kb = DFT_BLOCK_ROWS // s2
    out = pl.pallas_call(
        functools.partial(_dft_seq2_kernel, kb=kb, s2=s2),
        grid=(b, s1 // kb),
        in_specs=[
            pl.BlockSpec((kb, 2 * s2, 2 * s2), lambda i, j: (j, 0, 0)),
            pl.BlockSpec((2 * F_GROUP_DIM, F_GROUP_DIM), lambda i, j: (0, 0)),
            pl.BlockSpec((kb * s2, kb * s2), lambda i, j: (0, 0)),
            pl.BlockSpec((1, 2, kb, s2, F_W), lambda i, j: (i, 0, j, 0, 0)),
        ],
        out_specs=pl.BlockSpec((1, s2, kb, F_W), lambda i, j: (i, 0, j, 0)),
        out_shape=jax.ShapeDtypeStruct((b, s2, s1, F_W), BF16),
        compiler_params=pltpu.CompilerParams(
            dimension_semantics=("arbitrary", "arbitrary"),
            vmem_limit_bytes=_vmem_limit(2 * 3 * kb * s2 * F_W * 2 + 4 * kb * s2 * F_W * 4
                                         + 2 * (kb * s2) ** 2 * 2)),
        name="dft_seq2",
    )(g2, cs, perm, a)
    return out.reshape(b * s, F_W)


def _conv_kernel(prev_ref, cur_ref, next_ref, dw_ref, db_ref, lg_ref, lb_ref, o_ref, pad_ref, conv_ref):
    i = pl.program_id(1)
    last = pl.num_programs(1) - 1
    prev = jnp.where(i > 0, prev_ref[0, 0], 0.0)
    nxt = jnp.where(i < last, next_ref[0, 0], 0.0)
    n_lt = C_W // V7X_LANES
    lane_tiles = [slice(c * V7X_LANES, (c + 1) * V7X_LANES) for c in range(n_lt)]
    for c, ls in enumerate(lane_tiles):
        pad_ref[c, 0:CONV_HALO, :] = prev[:, ls]
        pad_ref[c, CONV_HALO:CONV_HALO + CONV_ROWS, :] = cur_ref[0, :, ls]
        pad_ref[c, CONV_HALO + CONV_ROWS:, :] = nxt[:, ls]
    first = CONV_HALO - CONV_PAD
    n_sub = CONV_CHUNK // V7X_SUBLANES
    for c, ls in enumerate(lane_tiles):
        w = [dw_ref[t, :, ls] for t in range(CONV_K)]
        bias = jnp.broadcast_to(db_ref[:, ls], (V7X_SUBLANES, V7X_LANES))

        def chunk_body(ch, carry, c=c, w=w, bias=bias):
            r0 = pl.multiple_of(ch * CONV_CHUNK, CONV_CHUNK)
            acc = [bias for _ in range(n_sub)]
            for t in range(CONV_K):
                for k in range(n_sub):
                    row = r0 + (k * V7X_SUBLANES + first + t)
                    acc[k] = acc[k] + w[t] * pad_ref[c, pl.ds(row, V7X_SUBLANES), :]
            conv_ref[c, pl.ds(r0, CONV_CHUNK), :] = jnp.concatenate(acc, axis=0)
            return carry

        lax.fori_loop(0, CONV_ROWS // CONV_CHUNK, chunk_body, 0)

    for ch in range(CONV_ROWS // CONV_CHUNK):
        r0 = ch * CONV_CHUNK
        accs = [conv_ref[c, r0:r0 + CONV_CHUNK, :] for c in range(n_lt)]
        mu = sum(accs).sum(axis=-1, keepdims=True) * (1.0 / C_W)
        xcs = [a - mu for a in accs]
        var = sum(x * x for x in xcs).sum(axis=-1, keepdims=True) * (1.0 / C_W)
        rstd = lax.rsqrt(var + EPS)
        for c, ls in enumerate(lane_tiles):
            y = (xcs[c] * rstd) * lg_ref[:, ls] + lb_ref[:, ls]
            o_ref[0, r0:r0 + CONV_CHUNK, ls] = (y * jax.nn.sigmoid(y)).astype(BF16)


def _conv_module(glu, b, s, dw, db, lg, lb):
    nblk = s // CONV_ROWS
    per = CONV_ROWS // CONV_HALO
    x3 = glu.reshape(b, s, C_W)
    x4 = glu.reshape(b, s // CONV_HALO, CONV_HALO, C_W)
    dwb = jnp.broadcast_to(dw[:, None, :], (CONV_K, V7X_SUBLANES, C_W))
    vec = lambda i, j: (0, 0)
    out = pl.pallas_call(
        _conv_kernel,
        grid=(b, nblk),
        in_specs=[
            pl.BlockSpec((1, 1, CONV_HALO, C_W), lambda i, j: (i, jnp.maximum(j * per - 1, 0), 0, 0)),
            pl.BlockSpec((1, CONV_ROWS, C_W), lambda i, j: (i, j, 0)),
            pl.BlockSpec((1, 1, CONV_HALO, C_W),
                         lambda i, j: (i, jnp.minimum((j + 1) * per, s // CONV_HALO - 1), 0, 0)),
            pl.BlockSpec((CONV_K, V7X_SUBLANES, C_W), lambda i, j: (0, 0, 0)),
            pl.BlockSpec((1, C_W), vec),
            pl.BlockSpec((1, C_W), vec),
            pl.BlockSpec((1, C_W), vec),
        ],
        out_specs=pl.BlockSpec((1, CONV_ROWS, C_W), lambda i, j: (i, j, 0)),
        out_shape=jax.ShapeDtypeStruct((b, s, C_W), BF16),
        scratch_shapes=[pltpu.VMEM((C_W // V7X_LANES, CONV_ROWS + 2 * CONV_HALO, V7X_LANES), F32),
                        pltpu.VMEM((C_W // V7X_LANES, CONV_ROWS, V7X_LANES), F32)],
        compiler_params=pltpu.CompilerParams(
            dimension_semantics=("arbitrary", "arbitrary"),
            vmem_limit_bytes=_vmem_limit(4 * CONV_ROWS * C_W * 4 + 2 * CONV_K * V7X_SUBLANES * C_W * 4)),
        name="conv",
    )(x4, x3, x4, dwb, db, lg, lb)
    return out.reshape(b * s, C_W)


@functools.lru_cache(maxsize=None)
def _natten_index_tables():
    n_key_rows = K_GROUPS * Q_ROWS
    n_dr, n_dc = 2 * WIN_R - 1, 2 * WIN_C - 1
    i = np.arange(Q_ROWS)[:, None]
    rr = np.arange(n_key_rows)[None, :]
    lo = (np.zeros_like(i), i, np.full_like(i, n_key_rows - WIN_R))
    dr_off = (WIN_R - 1, WIN_R - 1 - Q_ROWS, WIN_R - 1 - 2 * Q_ROWS)
    row_sel = np.zeros((3, Q_ROWS, n_key_rows, n_dr), np.float32)
    for p in range(3):
        row_ok = (rr >= lo[p]) & (rr < lo[p] + WIN_R)
        dr = rr - i + dr_off[p]
        for a in range(Q_ROWS):
            for r in range(n_key_rows):
                if row_ok[a, r]:
                    row_sel[p, a, r, dr[a, r]] = 1.0
    qc = np.arange(GRID_W)[:, None]
    kc = np.arange(GRID_W)[None, :]
    start = np.clip(qc - WIN_C // 2, 0, GRID_W - WIN_C)
    col_ok = (kc >= start) & (kc < start + WIN_C)
    dc = kc - qc + (WIN_C - 1)
    col_sel = np.zeros((n_dc, GRID_W, GRID_W), np.float32)
    for a in range(GRID_W):
        for c in range(GRID_W):
            if col_ok[a, c]:
                col_sel[dc[a, c], a, c] = 1.0
    ok = (row_sel.sum(-1) > 0)[:, :, None, :, None] & col_ok[None, None, :, None, :]
    return row_sel, col_sel, ok.reshape(3, Q_TOK, n_key_rows * GRID_W)


def _natten_bias_table(rpb):
    row_sel, col_sel, ok = _natten_index_tables()
    tab = jnp.einsum("pird,hdc,cqk->phiqrk", row_sel, rpb, col_sel, precision=lax.Precision.HIGHEST)
    tab = tab.reshape(3, N_HEADS, Q_TOK, K_GROUPS * Q_TOK)
    return jnp.where(ok[:, None], tab * LOG2E, NEG_INF)


def _natten_kernel(q_ref, kt0_ref, kt1_ref, kt2_ref, v0_ref, v1_ref, v2_ref, tab_ref, o_ref):
    kt_refs = (kt0_ref, kt1_ref, kt2_ref)
    v_refs = (v0_ref, v1_ref, v2_ref)
    j = pl.program_id(1)
    pat = jnp.where(j == 0, 0, jnp.where(j == pl.num_programs(1) - 1, 2, 1))

    def scores(hd):
        qh = q_ref[:, hd * HEAD_DIM:(hd + 1) * HEAD_DIM]
        return [jnp.dot(qh, kt_refs[g][hd], preferred_element_type=F32)
                + tab_ref[pat, hd, :, g * Q_TOK:(g + 1) * Q_TOK] for g in range(K_GROUPS)]

    def fold_lanes(parts, op):
        acc = parts[0]
        for part in parts[1:]:
            acc = op(acc, part)
        tiles = [acc[:, c * V7X_LANES:(c + 1) * V7X_LANES] for c in range(acc.shape[1] // V7X_LANES)]
        out = tiles[0]
        for tile in tiles[1:]:
            out = op(out, tile)
        return out

    sc_next = scores(0)
    for hd in range(N_HEADS):
        hs = slice(hd * HEAD_DIM, (hd + 1) * HEAD_DIM)
        sc = sc_next
        if hd + 1 < N_HEADS:
            sc_next = scores(hd + 1)
        m = fold_lanes(sc, jnp.maximum).max(axis=-1, keepdims=True)
        p = [jnp.exp2(s - m) for s in sc]
        denom = fold_lanes(p, jnp.add).sum(axis=-1, keepdims=True)
        o = jnp.dot(p[0].astype(BF16), v_refs[0][:, hs], preferred_element_type=F32)
        for g in range(1, K_GROUPS):
            o += jnp.dot(p[g].astype(BF16), v_refs[g][:, hs], preferred_element_type=F32)
        o_ref[:, hs] = (o * (1.0 / denom)).astype(BF16)


def _natten(q, kt, v, tab, b, s):
    ng = s // Q_TOK
    assert ng >= K_GROUPS

    def q_map(i, j):
        return (i * ng + j, 0)

    def window_block(i, j, off):
        return i * ng + jnp.clip(j - 1, 0, ng - K_GROUPS) + off

    blk = (Q_TOK, A_W)
    tab_shape = (3, N_HEADS, Q_TOK, K_GROUPS * Q_TOK)
    vmem = 2 * (2 + 2 * K_GROUPS) * Q_TOK * A_W * 2 + math.prod(tab_shape) * 4
    return pl.pallas_call(
        _natten_kernel,
        grid=(b, ng),
        in_specs=[pl.BlockSpec(blk, q_map)]
        + [pl.BlockSpec((None, N_HEADS, HEAD_DIM, Q_TOK), lambda i, j, g=g: (window_block(i, j, g), 0, 0, 0))
           for g in range(K_GROUPS)]
        + [pl.BlockSpec(blk, lambda i, j, g=g: (window_block(i, j, g), 0)) for g in range(K_GROUPS)]
        + [pl.BlockSpec(tab_shape, lambda i, j: (0, 0, 0, 0), pipeline_mode=pl.Buffered(1))],
        out_specs=pl.BlockSpec(blk, q_map),
        out_shape=jax.ShapeDtypeStruct((b * s, A_W), BF16),
        compiler_params=pltpu.CompilerParams(
            dimension_semantics=("arbitrary", "arbitrary"), vmem_limit_bytes=_vmem_limit(vmem)),
        name="natten",
    )(q, kt, kt, kt, v, v, v, tab)


def _merge_kernel(x_ref, h_ref, fm_ref, uc_ref, at_ref, wg0_ref, wg1_ref, wg2_ref,
                  bg0_ref, bg1_ref, bg2_ref, wf_ref, wc_ref, wa_ref, wo_ref, o_ref):
    def gate(w_ref, b_ref):
        return jax.nn.sigmoid(jnp.dot(h_ref[...], w_ref[...], preferred_element_type=F32) + b_ref[...])

    def step(acc_ref):
        merged = gate(wg0_ref, bg0_ref) * jnp.dot(fm_ref[...], wf_ref[...], preferred_element_type=F32)
        merged += gate(wg1_ref, bg1_ref) * jnp.dot(uc_ref[...], wc_ref[...], preferred_element_type=F32)
        merged += gate(wg2_ref, bg2_ref) * jnp.dot(at_ref[...], wa_ref[...], preferred_element_type=F32)
        o_ref[...] = acc_ref[...] + jnp.dot(merged.astype(BF16), wo_ref[...], preferred_element_type=F32)

    first = pl.program_id(1) == 0
    pl.when(first)(functools.partial(step, x_ref))
    pl.when(jnp.logical_not(first))(functools.partial(step, o_ref))


def _merge(x, layer, h, fm, uc, at, w_in, b_gate, w_f, w_c, w_a, w_o):
    t = x.shape[0]
    tm, tn = TOKEN_TILE, MERGE_COLS
    nj = D_MODEL // tn
    gate_blk = OFF_G // tn
    row = lambda i, j: (i, 0)
    col = lambda i, j: (layer, 0, j)
    gate_specs = [pl.BlockSpec((None, D_MODEL, tn), lambda i, j, br=br: (layer, 0, gate_blk + br * nj + j))
                  for br in range(N_BRANCH)]
    bias_specs = [pl.BlockSpec((1, tn), lambda i, j, br=br: (0, br * nj + j)) for br in range(N_BRANCH)]
    vmem = (4 * tm * D_MODEL * 4 + 2 * tm * (D_MODEL + F_W + C_W + A_W) * 2
            + 2 * (3 * D_MODEL + F_W + C_W + A_W + D_MODEL) * tn * 2 + 6 * tm * tn * 4)
    return pl.pallas_call(
        _merge_kernel,
        grid=(t // tm, nj),
        in_specs=[
            pl.BlockSpec((tm, D_MODEL), row),
            pl.BlockSpec((tm, D_MODEL), row),
            pl.BlockSpec((tm, F_W), row),
            pl.BlockSpec((tm, C_W), row),
            pl.BlockSpec((tm, A_W), row),
            *gate_specs,
            *bias_specs,
            pl.BlockSpec((None, F_W, tn), col),
            pl.BlockSpec((None, C_W, tn), col),
            pl.BlockSpec((None, A_W, tn), col),
            pl.BlockSpec((None, tn, D_MODEL), lambda i, j: (layer, j, 0)),
        ],
        out_specs=pl.BlockSpec((tm, D_MODEL), row),
        out_shape=jax.ShapeDtypeStruct((t, D_MODEL), F32),
        compiler_params=pltpu.CompilerParams(
            dimension_semantics=("arbitrary", "arbitrary"), vmem_limit_bytes=_vmem_limit(vmem)),
        name="merge",
    )(x, h, fm, uc, at, w_in, w_in, w_in, b_gate, b_gate, b_gate, w_f, w_c, w_a, w_o)


def _mlp_kernel(x_ref, g2_ref, w1_ref, w2_ref, o_ref, h_ref):
    def step(acc_ref):
        a = jnp.maximum(jnp.dot(h_ref[...], w1_ref[...], preferred_element_type=F32), 0.0)
        o_ref[...] = acc_ref[...] + jnp.dot((a * a).astype(BF16), w2_ref[...], preferred_element_type=F32)

    @pl.when(pl.program_id(1) == 0)
    def _():
        h_ref[...] = _rms_norm_bf16(x_ref[...], g2_ref[...])
        step(x_ref)

    pl.when(pl.program_id(1) > 0)(functools.partial(step, o_ref))


def _mlp(x, layer, g2, w1, w2):
    t = x.shape[0]
    tm, tf = TOKEN_TILE, MLP_COLS
    row = lambda i, j: (i, 0)
    vmem = 4 * tm * D_MODEL * 4 + tm * D_MODEL * 2 + 2 * 2 * D_MODEL * tf * 2 + 2 * tm * tf * 4
    return pl.pallas_call(
        _mlp_kernel,
        grid=(t // tm, D_FF // tf),
        in_specs=[
            pl.BlockSpec((tm, D_MODEL), row),
            pl.BlockSpec((1, D_MODEL), lambda i, j: (0, 0)),
            pl.BlockSpec((None, D_MODEL, tf), lambda i, j: (layer, 0, j)),
            pl.BlockSpec((None, tf, D_MODEL), lambda i, j: (layer, j, 0)),
        ],
        out_specs=pl.BlockSpec((tm, D_MODEL), row),
        out_shape=jax.ShapeDtypeStruct((t, D_MODEL), F32),
        scratch_shapes=[pltpu.VMEM((tm, D_MODEL), BF16)],
        compiler_params=pltpu.CompilerParams(
            dimension_semantics=("arbitrary", "arbitrary"), vmem_limit_bytes=_vmem_limit(vmem)),
        name="mlp",
    )(x, g2, w1, w2)


def _layer(x, b, s, l, w, p):
    _, s2 = _dft_factors(s)
    h, x3, glu, q, kt, v = _in_proj(x, l, p["n1"], w["w_in"], p["qg"], p["kg"], s2)
    fm = _fourier_mix(x3, b, s)
    uc = _conv_module(glu, b, s, p["dw"], p["db"], p["ln_g"], p["ln_b"])
    at = _natten(q, kt, v, p["tab"], b, s)
    x = _merge(x, l, h, fm, uc, at, w["w_in"], p["b_g"], w["w_f"], w["w_c"], w["w_a"], w["w_o"])
    return _mlp(x, l, p["n2"], w["w1"], w["w2"])


def kernel(x_prompt, x_sample, norm1_g, w_in, b_gate, w_fourier, conv_dw, conv_db, conv_ln_g, conv_ln_b,
           w_conv_out, q_norm_g, k_norm_g, rpb, w_attn_out, w_out, norm2_g, w_mlp_in, w_mlp_out):
    depth = w_in.shape[0]
    w = dict(w_in=w_in.astype(BF16), w_f=w_fourier.astype(BF16), w_c=w_conv_out.astype(BF16),
             w_a=w_attn_out.astype(BF16), w_o=w_out.astype(BF16),
             w1=w_mlp_in.astype(BF16), w2=w_mlp_out.astype(BF16))
    layers = []
    for l in range(depth):
        layers.append(dict(
            n1=norm1_g[l][None, :], n2=norm2_g[l][None, :], b_g=b_gate[l][None, :],
            dw=conv_dw[l], db=conv_db[l][None, :], ln_g=conv_ln_g[l][None, :], ln_b=conv_ln_b[l][None, :],
            qg=q_norm_g[l][None, :], kg=k_norm_g[l][None, :],
            tab=_natten_bias_table(rpb[l]),
        ))

    def trunk(x):
        b, s, d = x.shape
        y = x.reshape(b * s, d)
        for l, p in enumerate(layers):
            y = _layer(y, b, s, l, w, p)
        return y.reshape(b, s, d)

    return (trunk(x_prompt), trunk(x_sample))
```
